```python
import jax, jax.numpy as jnp
from jax import lax
import numpy as np

D_MODEL = 1024
BATCH = 1
SEQ = 16384
DEPTH = 1
DEC_BATCH = 8
DEC_SEQ = 16
PAST_LEN = 1024

CHUNK = 64
LEFT_CHUNKS = 8
BAND_CHUNKS = LEFT_CHUNKS + 1
BAND_PAST = LEFT_CHUNKS * CHUNK
D_MIX = D_MODEL
D_ATTN = D_MIX // 2
D_CONV = D_MIX - D_ATTN
ATT_HEADS = 8
ATT_HEAD_DIM = D_ATTN // ATT_HEADS
REL_CLIP = 128
CONV_WIDTH = 31
CONV_GROUPS = 8
MEM_TOKENS = 256
MEM_HEADS = 4
MEM_HEAD_DIM = D_MODEL // MEM_HEADS
D_FF = 4 * D_MODEL
EPS = 1e-6
NEG_INF = -1e30

kernel_name = "hybrid_chunk_band_attn_conformer_conv_step"


def rms_norm(x, g):
    x32 = x.astype(jnp.float32)
    y = x32 * lax.rsqrt(jnp.mean(x32 * x32, axis=-1, keepdims=True) + EPS)
    return (y * g.astype(jnp.float32)).astype(x.dtype)


def group_layer_norm(u, g, b):
    shp = u.shape
    u32 = u.astype(jnp.float32).reshape(shp[:-1] + (CONV_GROUPS, D_CONV // CONV_GROUPS))
    mu = jnp.mean(u32, axis=-1, keepdims=True)
    var = jnp.mean(jnp.square(u32 - mu), axis=-1, keepdims=True)
    y = ((u32 - mu) * lax.rsqrt(var + EPS)).reshape(shp)
    return (y * g.astype(jnp.float32) + b.astype(jnp.float32)).astype(u.dtype)


def rel_bias(table, q_off, k_off):
    d = q_off[:, None] - k_off[None, :]
    idx = jnp.clip(d, -REL_CLIP, REL_CLIP) + REL_CLIP
    return table[:, idx]


def in_proj(xn, w_in):
    B, T, _ = xn.shape
    z = xn @ w_in
    q, k, v, ga, gb = jnp.split(z, [D_ATTN, 2 * D_ATTN, 3 * D_ATTN, 3 * D_ATTN + D_CONV], axis=-1)
    heads = lambda t: t.reshape(B, T, ATT_HEADS, ATT_HEAD_DIM)
    u = ga * jax.nn.sigmoid(gb)
    return heads(q), heads(k), heads(v), u


def band_attn_prompt(q, k, v, table):
    B, S, H, Dh = q.shape
    NC = S // CHUNK
    pad = ((0, 0), (BAND_PAST, 0), (0, 0), (0, 0))
    kp = jnp.pad(k, pad)
    vp = jnp.pad(v, pad)

    def band(t):
        return jnp.stack(
            [t[:, j * CHUNK: j * CHUNK + S].reshape(B, NC, CHUNK, H, Dh) for j in range(BAND_CHUNKS)],
            axis=2).reshape(B, NC, BAND_CHUNKS * CHUNK, H, Dh)

    kb, vb = band(kp), band(vp)
    qc = q.reshape(B, NC, CHUNK, H, Dh)
    s = jnp.einsum('bnqhd,bnkhd->bnhqk', qc, kb).astype(jnp.float32) * (Dh ** -0.5)
    q_off = jnp.arange(CHUNK) + BAND_PAST
    k_off = jnp.arange(BAND_CHUNKS * CHUNK)
    s = s + rel_bias(table, q_off, k_off).astype(jnp.float32)[None, None]
    chunk_valid = (jnp.arange(NC)[:, None] + jnp.arange(BAND_CHUNKS)[None, :]) >= LEFT_CHUNKS
    key_valid = jnp.repeat(chunk_valid, CHUNK, axis=1)
    s = jnp.where(key_valid[None, :, None, None, :], s, NEG_INF)
    p = jax.nn.softmax(s, axis=-1).astype(v.dtype)
    o = jnp.einsum('bnhqk,bnkhd->bnqhd', p, vb)
    return o.reshape(B, S, H * Dh)


def band_attn_sample(q, k_all, v_all, table):
    B, T, H, Dh = q.shape
    L = k_all.shape[1]
    past = L - T
    s = jnp.einsum('bqhd,bkhd->bhqk', q, k_all).astype(jnp.float32) * (Dh ** -0.5)
    s = s + rel_bias(table, jnp.arange(T) + past, jnp.arange(L)).astype(jnp.float32)[None]
    p = jax.nn.softmax(s, axis=-1).astype(v_all.dtype)
    o = jnp.einsum('bhqk,bkhd->bqhd', p, v_all)
    return o.reshape(B, T, H * Dh)


def conv_branch(u_ext, w_dw, b_dw, ln_g, ln_b):
    c = lax.conv_general_dilated(
        u_ext, w_dw[:, None, :], window_strides=(1,), padding='VALID',
        dimension_numbers=('NWC', 'WIO', 'NWC'), feature_group_count=D_CONV) + b_dw
    return jax.nn.silu(group_layer_norm(c, ln_g, ln_b))


def merge_out(a, c, g_attn_out, g_conv_out, w_out):
    return jnp.concatenate([rms_norm(a, g_attn_out), rms_norm(c, g_conv_out)], axis=-1) @ w_out


def mem_kv(mem, g_mem, wk, wv):
    B, M, _ = mem.shape
    mn = rms_norm(mem, g_mem)
    return ((mn @ wk).reshape(B, M, MEM_HEADS, MEM_HEAD_DIM),
            (mn @ wv).reshape(B, M, MEM_HEADS, MEM_HEAD_DIM))


def mem_attn(hn, mk, mv, wq, wo):
    B, T, _ = hn.shape
    q = (hn @ wq).reshape(B, T, MEM_HEADS, MEM_HEAD_DIM)
    s = jnp.einsum('bqhd,bmhd->bhqm', q, mk).astype(jnp.float32) * (MEM_HEAD_DIM ** -0.5)
    p = jax.nn.softmax(s, axis=-1).astype(mv.dtype)
    o = jnp.einsum('bhqm,bmhd->bqhd', p, mv).reshape(B, T, MEM_HEADS * MEM_HEAD_DIM)
    return o @ wo


def sq_relu_mlp(hn, w_up, w_down):
    return jnp.square(jax.nn.relu(hn @ w_up)) @ w_down


def setup_inputs(seed: int = 0) -> dict:
    key = jax.random.key(seed)
    ks = iter(jax.random.split(key, 40))
    nrm = lambda shape, scale: jax.random.normal(next(ks), shape, jnp.float32) * scale
    gain = lambda n: 1.0 + nrm((DEPTH, n), 0.02)
    keep = min(BAND_PAST, PAST_LEN)
    return {
        "x_prompt": nrm((BATCH, SEQ, D_MODEL), 1.0),
        "x_sample": nrm((DEC_BATCH, DEC_SEQ, D_MODEL), 1.0),
        "mem_prompt": nrm((BATCH, MEM_TOKENS, D_MODEL), 1.0),
        "cache_attn_k": nrm((DEPTH, DEC_BATCH, keep, ATT_HEADS, ATT_HEAD_DIM), 1.0),
        "cache_attn_v": nrm((DEPTH, DEC_BATCH, keep, ATT_HEADS, ATT_HEAD_DIM), 1.0),
        "cache_conv": nrm((DEPTH, DEC_BATCH, CONV_WIDTH - 1, D_CONV), 1.0),
        "cache_mem_k": nrm((DEPTH, DEC_BATCH, MEM_TOKENS, MEM_HEADS, MEM_HEAD_DIM), 1.0),
        "cache_mem_v": nrm((DEPTH, DEC_BATCH, MEM_TOKENS, MEM_HEADS, MEM_HEAD_DIM), 1.0),
        "g_mix_pre": gain(D_MODEL),
        "w_in": nrm((DEPTH, D_MODEL, 3 * D_ATTN + 2 * D_CONV), D_MODEL ** -0.5),
        "att_rel_bias": nrm((DEPTH, ATT_HEADS, 2 * REL_CLIP + 1), 0.1),
        "w_dw": nrm((DEPTH, CONV_WIDTH, D_CONV), CONV_WIDTH ** -0.5),
        "b_dw": nrm((DEPTH, D_CONV), 0.02),
        "conv_ln_g": gain(D_CONV),
        "conv_ln_b": nrm((DEPTH, D_CONV), 0.02),
        "g_attn_out": gain(D_ATTN),
        "g_conv_out": gain(D_CONV),
        "w_out": nrm((DEPTH, D_MIX, D_MODEL), D_MIX ** -0.5),
        "g_mix_post": gain(D_MODEL),
        "g_mem_pre": gain(D_MODEL),
        "g_mem_kv": gain(D_MODEL),
        "w_mem_q": nrm((DEPTH, D_MODEL, MEM_HEADS * MEM_HEAD_DIM), D_MODEL ** -0.5),
        "w_mem_k": nrm((DEPTH, D_MODEL, MEM_HEADS * MEM_HEAD_DIM), D_MODEL ** -0.5),
        "w_mem_v": nrm((DEPTH, D_MODEL, MEM_HEADS * MEM_HEAD_DIM), D_MODEL ** -0.5),
        "w_mem_o": nrm((DEPTH, MEM_HEADS * MEM_HEAD_DIM, D_MODEL), D_MODEL ** -0.5),
        "g_mem_post": gain(D_MODEL),
        "g_ffn_pre": gain(D_MODEL),
        "w_ffn_up": nrm((DEPTH, D_MODEL, D_FF), D_MODEL ** -0.5),
        "w_ffn_down": nrm((DEPTH, D_FF, D_MODEL), D_FF ** -0.5),
        "g_ffn_post": gain(D_MODEL),
    }


def reference(x_prompt, x_sample, mem_prompt, cache_attn_k, cache_attn_v, cache_conv, cache_mem_k, cache_mem_v,
              g_mix_pre, w_in, att_rel_bias, w_dw, b_dw, conv_ln_g, conv_ln_b, g_attn_out, g_conv_out, w_out,
              g_mix_post, g_mem_pre, g_mem_kv, w_mem_q, w_mem_k, w_mem_v, w_mem_o, g_mem_post,
              g_ffn_pre, w_ffn_up, w_ffn_down, g_ffn_post):
    xp, xs = x_prompt, x_sample
    S = xp.shape[1]
    T = xs.shape[1]
    keep_p = min(BAND_PAST, S)
    p_k, p_v, p_c, p_mk, p_mv, s_k, s_v, s_c = [], [], [], [], [], [], [], []
    for l in range(DEPTH):
        q, k, v, u = in_proj(rms_norm(xp, g_mix_pre[l]), w_in[l])
        a = band_attn_prompt(q, k, v, att_rel_bias[l])
        u_ext = jnp.pad(u, ((0, 0), (CONV_WIDTH - 1, 0), (0, 0)))
        c = conv_branch(u_ext, w_dw[l], b_dw[l], conv_ln_g[l], conv_ln_b[l])
        xp = xp + rms_norm(merge_out(a, c, g_attn_out[l], g_conv_out[l], w_out[l]), g_mix_post[l])
        p_k.append(k[:, S - keep_p:])
        p_v.append(v[:, S - keep_p:])
        p_c.append(u[:, S - (CONV_WIDTH - 1):])
        q, k, v, u = in_proj(rms_norm(xs, g_mix_pre[l]), w_in[l])
        k_all = jnp.concatenate([cache_attn_k[l], k], axis=1)
        v_all = jnp.concatenate([cache_attn_v[l], v], axis=1)
        a = band_attn_sample(q, k_all, v_all, att_rel_bias[l])
        u_ext = jnp.concatenate([cache_conv[l], u], axis=1)
        c = conv_branch(u_ext, w_dw[l], b_dw[l], conv_ln_g[l], conv_ln_b[l])
        xs = xs + rms_norm(merge_out(a, c, g_attn_out[l], g_conv_out[l], w_out[l]), g_mix_post[l])
        s_k.append(k_all[:, T:])
        s_v.append(v_all[:, T:])
        s_c.append(u_ext[:, T:])
        mk, mv = mem_kv(mem_prompt, g_mem_kv[l], w_mem_k[l], w_mem_v[l])
        p_mk.append(mk)
        p_mv.append(mv)
        xp = xp + rms_norm(mem_attn(rms_norm(xp, g_mem_pre[l]), mk, mv, w_mem_q[l], w_mem_o[l]), g_mem_post[l])
        xs = xs + rms_norm(mem_attn(rms_norm(xs, g_mem_pre[l]), cache_mem_k[l], cache_mem_v[l],
                                    w_mem_q[l], w_mem_o[l]), g_mem_post[l])
        xp = xp + rms_norm(sq_relu_mlp(rms_norm(xp, g_ffn_pre[l]), w_ffn_up[l], w_ffn_down[l]), g_ffn_post[l])
        xs = xs + rms_norm(sq_relu_mlp(rms_norm(xs, g_ffn_pre[l]), w_ffn_up[l], w_ffn_down[l]), g_ffn_post[l])
    return (xp, xs, jnp.stack(p_k), jnp.stack(p_v), jnp.stack(p_c), jnp.stack(p_mk), jnp.stack(p_mv),
            jnp.stack(s_k), jnp.stack(s_v), jnp.stack(s_c))
```

```python
import functools

import jax
import jax.numpy as jnp
import numpy as np
from jax import lax
from jax.experimental import pallas as pl
from jax.experimental.pallas import tpu as pltpu

F32 = jnp.float32
BF16 = jnp.bfloat16

D_MODEL = 1024
CHUNK = 64
LEFT_CHUNKS = 8
BAND_CHUNKS = LEFT_CHUNKS + 1
BAND_PAST = LEFT_CHUNKS * CHUNK
BAND_KEYS = BAND_CHUNKS * CHUNK
D_ATTN = 512
D_CONV = 512
ATT_HEADS = 8
ATT_HEAD_DIM = 64
REL_CLIP = 128
LANES = 128
CONV_WIDTH = 31
CONV_GROUPS = 8
MEM_TOKENS = 256
MEM_HEADS = 4
MEM_HEAD_DIM = 256
D_FF = 4096
EPS = 1e-6
NEG_INF = -1e30

ROW_TILE = 512
TAIL_ROWS = 32
FF_CHUNK = 1024
VMEM_LIMIT_BYTES = 56 * 1024 * 1024


def _params(n_axes=1):
    return pltpu.CompilerParams(dimension_semantics=("arbitrary",) * n_axes,
                                vmem_limit_bytes=VMEM_LIMIT_BYTES)


def _const_spec(shape):
    zeros = (0,) * len(shape)
    return pl.BlockSpec(shape, lambda *_: zeros, pipeline_mode=pl.Buffered(1))


def _rms(x, g):
    ms = jnp.mean(x * x, axis=-1, keepdims=True)
    return x * lax.rsqrt(ms + EPS) * g


def _softmax_pv(s, v):
    m = jnp.max(s, axis=-1, keepdims=True)
    e = jnp.exp(s - m)
    l = jnp.sum(e, axis=-1, keepdims=True)
    o = jnp.dot(e.astype(BF16), v, preferred_element_type=F32)
    return o / l


def _bias_kernel(t_ref, o_ref):
    h = pl.program_id(0)
    qi = lax.broadcasted_iota(jnp.int32, (CHUNK, BAND_KEYS), 0)
    kj = lax.broadcasted_iota(jnp.int32, (CHUNK, BAND_KEYS), 1)
    idx = jnp.clip(qi + BAND_PAST - kj, -REL_CLIP, REL_CLIP) + REL_CLIP
    lo = BAND_PAST - (BAND_KEYS - 1)
    first = max(lo, -REL_CLIP) + REL_CLIP

    def body(i, acc):
        return jnp.where(idx == i, t_ref[h, i], acc)

    o_ref[0] = lax.fori_loop(first, 2 * REL_CLIP + 1, body, jnp.zeros((CHUNK, BAND_KEYS), F32))


def _rel_bias(table):
    return pl.pallas_call(
        _bias_kernel,
        grid=(ATT_HEADS,),
        in_specs=[pl.BlockSpec(memory_space=pltpu.SMEM)],
        out_specs=pl.BlockSpec((1, CHUNK, BAND_KEYS), lambda h: (h, 0, 0)),
        out_shape=jax.ShapeDtypeStruct((ATT_HEADS, CHUNK, BAND_KEYS), F32),
        compiler_params=_params(),
        name="rel_bias",
    )(table)


def _in_proj_kernel(x_ref, g_ref, w_ref, q_ref, k_ref, v_ref, u_ref):
    xn = _rms(x_ref[...], g_ref[...]).astype(BF16)

    def proj(c):
        return jnp.dot(xn, w_ref[:, c * D_ATTN:(c + 1) * D_ATTN], preferred_element_type=F32)

    q_ref[...] = (proj(0) * (ATT_HEAD_DIM ** -0.5)).astype(BF16)
    k_ref[...] = proj(1)
    v_ref[...] = proj(2)
    u_ref[...] = proj(3) * jax.nn.sigmoid(proj(4))


def _in_proj(x, g, w, tm):
    rows = x.shape[0]
    n_out = w.shape[1]
    row = lambda n: pl.BlockSpec((tm, n), lambda i: (i, 0))
    return pl.pallas_call(
        _in_proj_kernel,
        grid=(rows // tm,),
        in_specs=[row(D_MODEL), _const_spec((1, D_MODEL)), _const_spec((D_MODEL, n_out))],
        out_specs=[row(D_ATTN), row(D_ATTN), row(D_ATTN), row(D_CONV)],
        out_shape=[jax.ShapeDtypeStruct((rows, D_ATTN), BF16),
                   jax.ShapeDtypeStruct((rows, D_ATTN), F32),
                   jax.ShapeDtypeStruct((rows, D_ATTN), F32),
                   jax.ShapeDtypeStruct((rows, D_CONV), F32)],
        compiler_params=_params(),
        name="in_proj",
    )(x, g, w)


def _band_attn(q, kb, vb, bias_ref, n_q, n_k, mask):
    outs = []
    for h in range(ATT_HEADS):
        sl = slice(h * ATT_HEAD_DIM, (h + 1) * ATT_HEAD_DIM)
        s = lax.dot_general(q[:, sl], kb[:, sl], (((1,), (1,)), ((), ())), preferred_element_type=F32)
        s = s + bias_ref[h, :n_q, :n_k]
        if mask is not None:
            s = jnp.where(mask, s, NEG_INF)
        outs.append(_softmax_pv(s, vb[:, sl]))
    return jnp.concatenate(outs, axis=-1)


def _conv_tail(c, gmat_ref, b_dw_ref, ln_g_ref, ln_b_ref):
    c = c + b_dw_ref[...]
    gm = gmat_ref[...]
    c_hi = c.astype(BF16)
    c_lo = (c - c_hi.astype(F32)).astype(BF16)
    mu = jnp.dot(c_hi, gm, preferred_element_type=F32) + jnp.dot(c_lo, gm, preferred_element_type=F32)
    d = c - mu
    var = jnp.dot((d * d).astype(BF16), gm, preferred_element_type=F32)
    y = d * lax.rsqrt(var + EPS) * ln_g_ref[...] + ln_b_ref[...]
    return y * jax.nn.sigmoid(y)


def _merge(a, c, x, g_a_ref, g_c_ref, w_out_ref, g_post_ref):
    an = _rms(a, g_a_ref[...]).astype(BF16)
    cn = _rms(c, g_c_ref[...]).astype(BF16)
    y = (jnp.dot(an, w_out_ref[:D_ATTN, :], preferred_element_type=F32)
         + jnp.dot(cn, w_out_ref[D_ATTN:, :], preferred_element_type=F32))
    return x + _rms(y, g_post_ref[...])


def _mix_prompt_kernel(q_ref, kp_ref, kc_ref, vp_ref, vc_ref, ut_ref, uc_ref, x_ref, bias_ref, gmat_ref,
                       w_dw_ref, b_dw_ref, ln_g_ref, ln_b_ref, g_a_ref, g_c_ref, w_out_ref, g_post_ref,
                       o_ref, k_sc, v_sc, a_sc, u_sc, c_sc):
    i = pl.program_id(0)

    k_sc[:ROW_TILE, :] = kp_ref[...].astype(BF16)
    k_sc[ROW_TILE:, :] = kc_ref[...].astype(BF16)
    v_sc[:ROW_TILE, :] = vp_ref[...].astype(BF16)
    v_sc[ROW_TILE:, :] = vc_ref[...].astype(BF16)

    col = lax.broadcasted_iota(jnp.int32, (CHUNK, BAND_KEYS), 1)

    def chunk_body(j, carry):
        r0 = pl.multiple_of(j * CHUNK, CHUNK)
        q = q_ref[pl.ds(r0, CHUNK), :]
        kb = k_sc[pl.ds(r0, BAND_KEYS), :]
        vb = v_sc[pl.ds(r0, BAND_KEYS), :]
        mask = jnp.logical_or(i > 0, col >= (LEFT_CHUNKS - j) * CHUNK)
        a_sc[pl.ds(r0, CHUNK), :] = _band_attn(q, kb, vb, bias_ref, CHUNK, BAND_KEYS, mask)
        return carry

    lax.fori_loop(0, ROW_TILE // CHUNK, chunk_body, 0)

    for cb in range(D_CONV // LANES):
        ls = slice(cb * LANES, (cb + 1) * LANES)
        u_sc[cb, :TAIL_ROWS, :] = jnp.where(i > 0, ut_ref[:, ls], 0.0)
        u_sc[cb, TAIL_ROWS:, :] = uc_ref[:, ls]
    off = TAIL_ROWS - (CONV_WIDTH - 1)

    def conv_body(r, carry):
        r0 = pl.multiple_of(r * CHUNK, CHUNK)
        for cb in range(D_CONV // LANES):
            ls = slice(cb * LANES, (cb + 1) * LANES)
            win = {}
            acc = [jnp.zeros((8, LANES), F32) for _ in range(CHUNK // 8)]
            for j in range(CONV_WIDTH):
                wj = w_dw_ref[j:j + 1, ls]
                for m in range(CHUNK // 8):
                    s = off + j + 8 * m
                    if s not in win:
                        win[s] = u_sc[cb, pl.ds(r0 + s, 8), :]
                    acc[m] = acc[m] + win[s] * wj
            for m in range(CHUNK // 8):
                c_sc[pl.ds(r0 + 8 * m, 8), ls] = acc[m]
        return carry

    lax.fori_loop(0, ROW_TILE // CHUNK, conv_body, 0)

    c = _conv_tail(c_sc[...], gmat_ref, b_dw_ref, ln_g_ref, ln_b_ref)
    o_ref[...] = _merge(a_sc[...], c, x_ref[...], g_a_ref, g_c_ref, w_out_ref, g_post_ref)


def _mix_prompt(q, k, v, u, x, bias, gmat, w_dw, b_dw, ln_g, ln_b, g_a, g_c, w_out, g_post):
    rows = x.shape[0]
    n_tiles = rows // ROW_TILE
    tails_per_tile = ROW_TILE // TAIL_ROWS
    cur = lambda n: pl.BlockSpec((ROW_TILE, n), lambda i: (i, 0))
    prev = lambda n: pl.BlockSpec((ROW_TILE, n), lambda i: (jnp.maximum(i - 1, 0), 0))
    tail = pl.BlockSpec((TAIL_ROWS, D_CONV), lambda i: (jnp.maximum(i * tails_per_tile - 1, 0), 0))
    vec = lambda n: _const_spec((1, n))
    return pl.pallas_call(
        _mix_prompt_kernel,
        grid=(n_tiles,),
        in_specs=[cur(D_ATTN), prev(D_ATTN), cur(D_ATTN), prev(D_ATTN), cur(D_ATTN), tail, cur(D_CONV),
                  cur(D_MODEL), _const_spec(bias.shape), _const_spec(gmat.shape),
                  _const_spec(w_dw.shape), vec(D_CONV), vec(D_CONV), vec(D_CONV), vec(D_ATTN), vec(D_CONV),
                  _const_spec(w_out.shape), vec(D_MODEL)],
        out_specs=cur(D_MODEL),
        out_shape=jax.ShapeDtypeStruct((rows, D_MODEL), F32),
        scratch_shapes=[pltpu.VMEM((2 * ROW_TILE, D_ATTN), BF16), pltpu.VMEM((2 * ROW_TILE, D_ATTN), BF16),
                        pltpu.VMEM((ROW_TILE, D_ATTN), F32), pltpu.VMEM((D_CONV // LANES, TAIL_ROWS + ROW_TILE, LANES), F32),
                        pltpu.VMEM((ROW_TILE, D_CONV), F32)],
        compiler_params=_params(),
        name="mix_prompt",
    )(q, k, k, v, v, u, u, x, bias, gmat, w_dw, b_dw, ln_g, ln_b, g_a, g_c, w_out, g_post)


def _mix_sample_kernel(q_ref, kn_ref, vn_ref, ck_ref, cv_ref, un_ref, cc_ref, x_ref, bias_ref, gmat_ref,
                       w_dw_ref, b_dw_ref, ln_g_ref, ln_b_ref, g_a_ref, g_c_ref, w_out_ref, g_post_ref,
                       o_ref, u_sc):
    t = q_ref.shape[0]
    past = ck_ref.shape[1]
    kb = jnp.concatenate([ck_ref[0], kn_ref[...]], axis=0).astype(BF16)
    vb = jnp.concatenate([cv_ref[0], vn_ref[...]], axis=0).astype(BF16)
    a = _band_attn(q_ref[...], kb, vb, bias_ref, t, past + t, None)

    u_sc[:CONV_WIDTH - 1, :] = cc_ref[0]
    u_sc[CONV_WIDTH - 1:, :] = un_ref[...]
    acc = jnp.zeros((t, D_CONV), F32)
    for j in range(CONV_WIDTH):
        acc = acc + u_sc[j:j + t, :] * w_dw_ref[j:j + 1, :]
    c = _conv_tail(acc, gmat_ref, b_dw_ref, ln_g_ref, ln_b_ref)
    o_ref[...] = _merge(a, c, x_ref[...], g_a_ref, g_c_ref, w_out_ref, g_post_ref)


def _mix_sample(q, k, v, u, x, cache_k, cache_v, cache_c, bias, gmat, w_dw, b_dw, ln_g, ln_b, g_a, g_c, w_out,
                g_post):
    n_b, past = cache_k.shape[0], cache_k.shape[1]
    t = x.shape[0] // n_b
    row = lambda n: pl.BlockSpec((t, n), lambda b: (b, 0))
    per_b = lambda a: pl.BlockSpec((1,) + a.shape[1:], lambda b: (b, 0, 0))
    vec = lambda n: _const_spec((1, n))
    return pl.pallas_call(
        _mix_sample_kernel,
        grid=(n_b,),
        in_specs=[row(D_ATTN), row(D_ATTN), row(D_ATTN), per_b(cache_k), per_b(cache_v), row(D_CONV),
                  per_b(cache_c), row(D_MODEL), _const_spec(bias.shape), _const_spec(gmat.shape),
                  _const_spec(w_dw.shape), vec(D_CONV), vec(D_CONV), vec(D_CONV), vec(D_ATTN), vec(D_CONV),
                  _const_spec(w_out.shape), vec(D_MODEL)],
        out_specs=row(D_MODEL),
        out_shape=jax.ShapeDtypeStruct(x.shape, F32),
        scratch_shapes=[pltpu.VMEM((CONV_WIDTH - 1 + t, D_CONV), F32)],
        compiler_params=_params(),
        name="mix_sample",
    )(q, k, v, cache_k, cache_v, u, cache_c, x, bias, gmat, w_dw, b_dw, ln_g, ln_b, g_a, g_c, w_out, g_post)


def _mem_kv_kernel(m_ref, g_ref, wk_ref, wv_ref, k_ref, v_ref):
    mn = _rms(m_ref[...], g_ref[...]).astype(BF16)
    k_ref[...] = jnp.dot(mn, wk_ref[...], preferred_element_type=F32)
    v_ref[...] = jnp.dot(mn, wv_ref[...], preferred_element_type=F32)


def _mem_kv(mem, g, wk, wv):
    rows = mem.shape[0]
    full = lambda a: _const_spec(a.shape)
    return pl.pallas_call(
        _mem_kv_kernel,
        grid=(1,),
        in_specs=[full(mem), full(g), full(wk), full(wv)],
        out_specs=[pl.BlockSpec((rows, D_MODEL), lambda i: (0, 0))] * 2,
        out_shape=[jax.ShapeDtypeStruct((rows, D_MODEL), F32)] * 2,
        compiler_params=_params(),
        name="mem_kv",
    )(mem, g, wk, wv)


def _mem_attn_kernel(x_ref, g_pre_ref, wq_ref, mk_ref, mv_ref, wo_ref, g_post_ref, o_ref):
    x = x_ref[...]
    hn = _rms(x, g_pre_ref[...]).astype(BF16)
    q = (jnp.dot(hn, wq_ref[...], preferred_element_type=F32) * (MEM_HEAD_DIM ** -0.5)).astype(BF16)
    mk = mk_ref[0].astype(BF16)
    mv = mv_ref[0].astype(BF16)
    outs = []
    for h in range(MEM_HEADS):
        sl = slice(h * MEM_HEAD_DIM, (h + 1) * MEM_HEAD_DIM)
        s = lax.dot_general(q[:, sl], mk[:, sl], (((1,), (1,)), ((), ())), preferred_element_type=F32)
        outs.append(_softmax_pv(s, mv[:, sl]).astype(BF16))
    o = jnp.concatenate(outs, axis=-1)
    y = jnp.dot(o, wo_ref[...], preferred_element_type=F32)
    o_ref[...] = x + _rms(y, g_post_ref[...])


def _mem_attn(x, g_pre, wq, mk, mv, wo, g_post, tm):
    rows = x.shape[0]
    n_tiles = rows // tm
    tiles_per_b = n_tiles // mk.shape[0]
    row = pl.BlockSpec((tm, D_MODEL), lambda i: (i, 0))
    mem = pl.BlockSpec((1, MEM_TOKENS, D_MODEL), lambda i: (i // tiles_per_b, 0, 0))
    return pl.pallas_call(
        _mem_attn_kernel,
        grid=(n_tiles,),
        in_specs=[row, _const_spec((1, D_MODEL)), _const_spec(wq.shape), mem, mem, _const_spec(wo.shape),
                  _const_spec((1, D_MODEL))],
        out_specs=row,
        out_shape=jax.ShapeDtypeStruct(x.shape, F32),
        compiler_params=_params(),
        name="mem_attn",
    )(x, g_pre, wq, mk, mv, wo, g_post)


def _mlp_kernel(x_ref, g_pre_ref, w_up_ref, w_down_ref, g_post_ref, o_ref):
    x = x_ref[...]
    hn = _rms(x, g_pre_ref[...]).astype(BF16)
    acc = jnp.zeros(x.shape, F32)
    for c in range(D_FF // FF_CHUNK):
        sl = slice(c * FF_CHUNK, (c + 1) * FF_CHUNK)
        h = jnp.maximum(jnp.dot(hn, w_up_ref[:, sl], preferred_element_type=F32), 0.0)
        acc = acc + jnp.dot((h * h).astype(BF16), w_down_ref[sl, :], preferred_element_type=F32)
    o_ref[...] = x + _rms(acc, g_post_ref[...])


def _mlp(x, g_pre, w_up, w_down, g_post, tm):
    rows = x.shape[0]
    row = pl.BlockSpec((tm, D_MODEL), lambda i: (i, 0))
    return pl.pallas_call(
        _mlp_kernel,
        grid=(rows // tm,),
        in_specs=[row, _const_spec((1, D_MODEL)), _const_spec(w_up.shape), _const_spec(w_down.shape),
                  _const_spec((1, D_MODEL))],
        out_specs=row,
        out_shape=jax.ShapeDtypeStruct(x.shape, F32),
        compiler_params=_params(),
        name="mlp",
    )(x, g_pre, w_up, w_down, g_post)


def _group_mean_matrix():
    g = D_CONV // CONV_GROUPS
    return jnp.asarray(np.kron(np.eye(CONV_GROUPS), np.full((g, g), 1.0 / g)), BF16)


def kernel(x_prompt, x_sample, mem_prompt, cache_attn_k, cache_attn_v, cache_conv, cache_mem_k, cache_mem_v, g_mix_pre, w_in, att_rel_bias, w_dw, b_dw, conv_ln_g, conv_ln_b, g_attn_out, g_conv_out, w_out, g_mix_post, g_mem_pre, g_mem_kv, w_mem_q, w_mem_k, w_mem_v, w_mem_o, g_mem_post, g_ffn_pre, w_ffn_up, w_ffn_down, g_ffn_post):
    n_bp, seq, _ = x_prompt.shape
    n_bs, t_new, _ = x_sample.shape
    depth = w_in.shape[0]
    assert n_bp == 1 and seq % ROW_TILE == 0 and seq >= BAND_PAST
    keep_s = cache_attn_k.shape[2]
    assert keep_s == BAND_PAST and t_new <= CHUNK

    xp = x_prompt.reshape(seq, D_MODEL)
    xs = x_sample.reshape(n_bs * t_new, D_MODEL)
    mem = mem_prompt.reshape(MEM_TOKENS, D_MODEL)
    gmat = _group_mean_matrix()
    vec = lambda a: a.reshape(1, -1)
    p_k, p_v, p_c, p_mk, p_mv, s_k, s_v, s_c = [], [], [], [], [], [], [], []

    for l in range(depth):
        w_in_b = w_in[l].astype(BF16)
        w_out_b = w_out[l].astype(BF16)
        wq_b, wk_b, wv_b, wo_b = (w[l].astype(BF16) for w in (w_mem_q, w_mem_k, w_mem_v, w_mem_o))
        w_up_b = w_ffn_up[l].astype(BF16)
        w_down_b = w_ffn_down[l].astype(BF16)
        bias = _rel_bias(att_rel_bias[l])
        mix_w = (bias, gmat, w_dw[l], vec(b_dw[l]), vec(conv_ln_g[l]), vec(conv_ln_b[l]), vec(g_attn_out[l]),
                 vec(g_conv_out[l]), w_out_b, vec(g_mix_post[l]))

        q, k, v, u = _in_proj(xp, vec(g_mix_pre[l]), w_in_b, ROW_TILE)
        xp = _mix_prompt(q, k, v, u, xp, *mix_w)
        p_k.append(k[seq - BAND_PAST:].reshape(n_bp, BAND_PAST, ATT_HEADS, ATT_HEAD_DIM))
        p_v.append(v[seq - BAND_PAST:].reshape(n_bp, BAND_PAST, ATT_HEADS, ATT_HEAD_DIM))
        p_c.append(u[seq - (CONV_WIDTH - 1):].reshape(n_bp, CONV_WIDTH - 1, D_CONV))

        ck = cache_attn_k[l].reshape(n_bs, keep_s, D_ATTN)
        cv = cache_attn_v[l].reshape(n_bs, keep_s, D_ATTN)
        q, k, v, u = _in_proj(xs, vec(g_mix_pre[l]), w_in_b, n_bs * t_new)
        xs = _mix_sample(q, k, v, u, xs, ck, cv, cache_conv[l], *mix_w)
        heads = lambda a: a.reshape(n_bs, t_new, ATT_HEADS, ATT_HEAD_DIM)
        s_k.append(jnp.concatenate([cache_attn_k[l], heads(k)], axis=1)[:, t_new:])
        s_v.append(jnp.concatenate([cache_attn_v[l], heads(v)], axis=1)[:, t_new:])
        s_c.append(jnp.concatenate([cache_conv[l], u.reshape(n_bs, t_new, D_CONV)], axis=1)[:, t_new:])

        mk, mv = _mem_kv(mem, vec(g_mem_kv[l]), wk_b, wv_b)
        p_mk.append(mk.reshape(n_bp, MEM_TOKENS, MEM_HEADS, MEM_HEAD_DIM))
        p_mv.append(mv.reshape(n_bp, MEM_TOKENS, MEM_HEADS, MEM_HEAD_DIM))
        mem_w = (vec(g_mem_pre[l]), wq_b)
        xp = _mem_attn(xp, *mem_w, mk[None], mv[None], wo_b, vec(g_mem_post[l]), ROW_TILE)
        xs = _mem_attn(xs, *mem_w, cache_mem_k[l].reshape(n_bs, MEM_TOKENS, D_MODEL),
                       cache_mem_v[l].reshape(n_bs, MEM_TOKENS, D_MODEL), wo_b, vec(g_mem_post[l]), t_new)

        ffn_w = (vec(g_ffn_pre[l]), w_up_b, w_down_b, vec(g_ffn_post[l]))
        xp = _mlp(xp, *ffn_w, ROW_TILE)
        xs = _mlp(xs, *ffn_w, n_bs * t_new)

    return (xp.reshape(x_prompt.shape), xs.reshape(x_sample.shape), jnp.stack(p_k), jnp.stack(p_v),
            jnp.stack(p_c), jnp.stack(p_mk), jnp.stack(p_mv), jnp.stack(s_k), jnp.stack(s_v), jnp.stack(s_c))
```

```python
import jax
import jax.numpy as jnp
import numpy as np
from jax import lax
from jax.experimental import pallas as pl
from jax.experimental.pallas import tpu as pltpu

F32 = jnp.float32
BF16 = jnp.bfloat16

D_MODEL = 1024
CHUNK = 64
LEFT_CHUNKS = 8
BAND_CHUNKS = LEFT_CHUNKS + 1
BAND_PAST = LEFT_CHUNKS * CHUNK
BAND_KEYS = BAND_CHUNKS * CHUNK
D_ATTN = 512
D_CONV = 512
ATT_HEADS = 8
ATT_HEAD_DIM = 64
REL_CLIP = 128
CONV_WIDTH = 31
CONV_GROUPS = 8
MEM_TOKENS = 256
MEM_HEADS = 4
MEM_HEAD_DIM = 256
D_FF = 4096
EPS = 1e-6
NEG_INF = -1e30

LANES = 128
ROW_TILE = 512
TAIL_ROWS = 32
FF_CHUNK = 1024
VMEM_LIMIT_BYTES = 56 * 1024 * 1024

PAIR_ROWS = 2 * CHUNK
PAIR_KEYS = BAND_KEYS + CHUNK
HEAD_PAIRS = ATT_HEADS // 2


def _params(n_axes=1):
    return pltpu.CompilerParams(dimension_semantics=("arbitrary",) * n_axes,
                                vmem_limit_bytes=VMEM_LIMIT_BYTES)


def _const_spec(shape):
    zeros = (0,) * len(shape)
    return pl.BlockSpec(shape, lambda *_: zeros, pipeline_mode=pl.Buffered(1))


def _rms(x, g):
    ms = jnp.mean(x * x, axis=-1, keepdims=True)
    return x * lax.rsqrt(ms + EPS) * g


def _softmax_pv(s, v):
    m = jnp.max(s, axis=-1, keepdims=True)
    e = jnp.exp(s - m)
    l = jnp.sum(e, axis=-1, keepdims=True)
    o = jnp.dot(e.astype(BF16), v, preferred_element_type=F32)
    return o / l


def _bias_kernel(t_ref, o_ref):
    p = pl.program_id(0)
    near = PAIR_KEYS - 2 * LANES
    row = lax.broadcasted_iota(jnp.int32, (PAIR_ROWS, 2 * LANES), 0)
    col = lax.broadcasted_iota(jnp.int32, (PAIR_ROWS, 2 * LANES), 1) + near
    idx = jnp.clip(row + BAND_PAST - col, -REL_CLIP, REL_CLIP) + REL_CLIP
    first = BAND_PAST - (BAND_KEYS - 1) + REL_CLIP
    band_lo = (row // CHUNK) * CHUNK
    in_band = jnp.logical_and(col >= band_lo, col < band_lo + BAND_KEYS)
    for half in range(2):
        h = 2 * p + half
        far = t_ref[h, 2 * REL_CLIP]

        def body(i, acc):
            return jnp.where(idx == i, t_ref[h, i], acc)

        tile = lax.fori_loop(first, 2 * REL_CLIP + 1, body, jnp.zeros((PAIR_ROWS, 2 * LANES), F32))
        rows = slice(half * PAIR_ROWS, (half + 1) * PAIR_ROWS)
        o_ref[0, rows, near:] = jnp.where(in_band, tile, NEG_INF)
        lo_row = lax.broadcasted_iota(jnp.int32, (PAIR_ROWS, near), 0)
        lo_col = lax.broadcasted_iota(jnp.int32, (PAIR_ROWS, near), 1)
        o_ref[0, rows, :near] = jnp.where(lo_col >= (lo_row // CHUNK) * CHUNK, far, NEG_INF)


def _rel_bias(table):
    return pl.pallas_call(
        _bias_kernel,
        grid=(HEAD_PAIRS,),
        in_specs=[pl.BlockSpec(memory_space=pltpu.SMEM)],
        out_specs=pl.BlockSpec((1, 2 * PAIR_ROWS, PAIR_KEYS), lambda p: (p, 0, 0)),
        out_shape=jax.ShapeDtypeStruct((HEAD_PAIRS, 2 * PAIR_ROWS, PAIR_KEYS), F32),
        compiler_params=_params(),
        name="rel_bias",
    )(table)


def _in_proj_kernel(x_ref, g_ref, w_ref, q_ref, kb_ref, vb_ref, u_ref, kt_ref, vt_ref):
    xn = _rms(x_ref[...], g_ref[...]).astype(BF16)

    def proj(c):
        return jnp.dot(xn, w_ref[:, c * D_ATTN:(c + 1) * D_ATTN], preferred_element_type=F32)

    q_ref[...] = (proj(0) * (ATT_HEAD_DIM ** -0.5)).astype(BF16)
    k = proj(1)
    v = proj(2)
    kb_ref[...] = k.astype(BF16)
    vb_ref[...] = v.astype(BF16)
    u_ref[...] = proj(3) * jax.nn.sigmoid(proj(4))

    @pl.when(pl.program_id(0) == pl.num_programs(0) - 1)
    def _():
        kt_ref[...] = k
        vt_ref[...] = v


def _in_proj(x, g, w, tm):
    rows = x.shape[0]
    n_out = w.shape[1]
    row = lambda n: pl.BlockSpec((tm, n), lambda i: (i, 0))
    last = pl.BlockSpec((tm, D_ATTN), lambda i: (0, 0))
    return pl.pallas_call(
        _in_proj_kernel,
        grid=(rows // tm,),
        in_specs=[row(D_MODEL), _const_spec((1, D_MODEL)), _const_spec((D_MODEL, n_out))],
        out_specs=[row(D_ATTN), row(D_ATTN), row(D_ATTN), row(D_CONV), last, last],
        out_shape=[jax.ShapeDtypeStruct((rows, D_ATTN), BF16),
                   jax.ShapeDtypeStruct((rows, D_ATTN), BF16),
                   jax.ShapeDtypeStruct((rows, D_ATTN), BF16),
                   jax.ShapeDtypeStruct((rows, D_CONV), F32),
                   jax.ShapeDtypeStruct((tm, D_ATTN), F32),
                   jax.ShapeDtypeStruct((tm, D_ATTN), F32)],
        compiler_params=_params(),
        name="in_proj",
    )(x, g, w)


def _pair_attn(q2, kw, vw, bias, key_lo):
    lane = lax.broadcasted_iota(jnp.int32, q2.shape, 1)
    zero = jnp.zeros_like(q2)
    lhs = jnp.concatenate([jnp.where(lane < ATT_HEAD_DIM, q2, zero), jnp.where(lane < ATT_HEAD_DIM, zero, q2)],
                          axis=0)
    s = lax.dot_general(lhs, kw, (((1,), (1,)), ((), ())), preferred_element_type=F32) + bias
    if key_lo is not None:
        col = lax.broadcasted_iota(jnp.int32, s.shape, 1)
        s = jnp.where(col >= key_lo, s, NEG_INF)
    o = _softmax_pv(s, vw)
    out_lane = lax.broadcasted_iota(jnp.int32, (PAIR_ROWS, LANES), 1)
    return jnp.where(out_lane < ATT_HEAD_DIM, o[:PAIR_ROWS], o[PAIR_ROWS:])


def _band_attn_small(q, kb, vb, bias_ref, n_q, n_k):
    outs = []
    for h in range(ATT_HEADS):
        sl = slice(h * ATT_HEAD_DIM, (h + 1) * ATT_HEAD_DIM)
        r0 = (h % 2) * PAIR_ROWS
        s = lax.dot_general(q[:, sl], kb[:, sl], (((1,), (1,)), ((), ())), preferred_element_type=F32)
        s = s + bias_ref[h // 2, r0:r0 + n_q, :n_k]
        outs.append(_softmax_pv(s, vb[:, sl]))
    return jnp.concatenate(outs, axis=-1)


def _conv_tail(c, gmat_ref, b_dw_ref, ln_g_ref, ln_b_ref):
    c = c + b_dw_ref[...]
    gm = gmat_ref[...]
    c_hi = c.astype(BF16)
    c_lo = (c - c_hi.astype(F32)).astype(BF16)
    mu = jnp.dot(c_hi, gm, preferred_element_type=F32) + jnp.dot(c_lo, gm, preferred_element_type=F32)
    d = c - mu
    var = jnp.dot((d * d).astype(BF16), gm, preferred_element_type=F32)
    y = d * lax.rsqrt(var + EPS) * ln_g_ref[...] + ln_b_ref[...]
    return y * jax.nn.sigmoid(y)


def _merge(a, c, x, g_a_ref, g_c_ref, w_out_ref, g_post_ref):
    an = _rms(a, g_a_ref[...]).astype(BF16)
    cn = _rms(c, g_c_ref[...]).astype(BF16)
    y = (jnp.dot(an, w_out_ref[:D_ATTN, :], preferred_element_type=F32)
         + jnp.dot(cn, w_out_ref[D_ATTN:, :], preferred_element_type=F32))
    return x + _rms(y, g_post_ref[...])


def _mix_prompt_kernel(q_ref, kp_ref, kc_ref, vp_ref, vc_ref, ut_ref, uc_ref, x_ref, bias_ref, gmat_ref,
                       w_dw_ref, b_dw_ref, ln_g_ref, ln_b_ref, g_a_ref, g_c_ref, w_out_ref, g_post_ref,
                       o_ref, k_sc, v_sc, a_sc, u_sc, c_sc):
    i = pl.program_id(0)

    k_sc[:ROW_TILE, :] = kp_ref[...]
    k_sc[ROW_TILE:, :] = kc_ref[...]
    v_sc[:ROW_TILE, :] = vp_ref[...]
    v_sc[ROW_TILE:, :] = vc_ref[...]

    def attend(first_tile):
        def pair_body(a, carry):
            r0 = pl.multiple_of(a * PAIR_ROWS, PAIR_ROWS)
            key_lo = ROW_TILE - r0 if first_tile else None
            for p in range(HEAD_PAIRS):
                ls = slice(p * LANES, (p + 1) * LANES)
                a_sc[pl.ds(r0, PAIR_ROWS), ls] = _pair_attn(
                    q_ref[pl.ds(r0, PAIR_ROWS), ls], k_sc[pl.ds(r0, PAIR_KEYS), ls],
                    v_sc[pl.ds(r0, PAIR_KEYS), ls], bias_ref[p], key_lo)
            return carry

        lax.fori_loop(0, ROW_TILE // PAIR_ROWS, pair_body, 0)

    pl.when(i == 0)(lambda: attend(True))
    pl.when(i > 0)(lambda: attend(False))

    for cb in range(D_CONV // LANES):
        ls = slice(cb * LANES, (cb + 1) * LANES)
        u_sc[cb, :TAIL_ROWS, :] = jnp.where(i > 0, ut_ref[:, ls], 0.0)
        u_sc[cb, TAIL_ROWS:, :] = uc_ref[:, ls]
    off = TAIL_ROWS - (CONV_WIDTH - 1)

    def conv_body(r, carry):
        r0 = pl.multiple_of(r * CHUNK, CHUNK)
        for cb in range(D_CONV // LANES):
            ls = slice(cb * LANES, (cb + 1) * LANES)
            win = {}
            acc = [jnp.zeros((8, LANES), F32) for _ in range(CHUNK // 8)]
            for j in range(CONV_WIDTH):
                wj = w_dw_ref[j:j + 1, ls]
                for m in range(CHUNK // 8):
                    s = off + j + 8 * m
                    if s not in win:
                        win[s] = u_sc[cb, pl.ds(r0 + s, 8), :]
                    acc[m] = acc[m] + win[s] * wj
            for m in range(CHUNK // 8):
                c_sc[pl.ds(r0 + 8 * m, 8), ls] = acc[m]
        return carry

    lax.fori_loop(0, ROW_TILE // CHUNK, conv_body, 0)

    c = _conv_tail(c_sc[...], gmat_ref, b_dw_ref, ln_g_ref, ln_b_ref)
    o_ref[...] = _merge(a_sc[...], c, x_ref[...], g_a_ref, g_c_ref, w_out_ref, g_post_ref)


def _mix_prompt(q, kb, vb, u, x, bias, gmat, w_dw, b_dw, ln_g, ln_b, g_a, g_c, w_out, g_post):
    rows = x.shape[0]
    n_tiles = rows // ROW_TILE
    tails_per_tile = ROW_TILE // TAIL_ROWS
    cur = lambda n: pl.BlockSpec((ROW_TILE, n), lambda i: (i, 0))
    prev = lambda n: pl.BlockSpec((ROW_TILE, n), lambda i: (jnp.maximum(i - 1, 0), 0))
    tail = pl.BlockSpec((TAIL_ROWS, D_CONV), lambda i: (jnp.maximum(i * tails_per_tile - 1, 0), 0))
    vec = lambda n: _const_spec((1, n))
    return pl.pallas_call(
        _mix_prompt_kernel,
        grid=(n_tiles,),
        in_specs=[cur(D_ATTN), prev(D_ATTN), cur(D_ATTN), prev(D_ATTN), cur(D_ATTN), tail, cur(D_CONV),
                  cur(D_MODEL), _const_spec(bias.shape), _const_spec(gmat.shape),
                  _const_spec(w_dw.shape), vec(D_CONV), vec(D_CONV), vec(D_CONV), vec(D_ATTN), vec(D_CONV),
                  _const_spec(w_out.shape), vec(D_MODEL)],
        out_specs=cur(D_MODEL),
        out_shape=jax.ShapeDtypeStruct((rows, D_MODEL), F32),
        scratch_shapes=[pltpu.VMEM((2 * ROW_TILE, D_ATTN), BF16), pltpu.VMEM((2 * ROW_TILE, D_ATTN), BF16),
                        pltpu.VMEM((ROW_TILE, D_ATTN), F32),
                        pltpu.VMEM((D_CONV // LANES, TAIL_ROWS + ROW_TILE, LANES), F32),
                        pltpu.VMEM((ROW_TILE, D_CONV), F32)],
        compiler_params=_params(),
        name="mix_prompt",
    )(q, kb, kb, vb, vb, u, u, x, bias, gmat, w_dw, b_dw, ln_g, ln_b, g_a, g_c, w_out, g_post)


def _mix_sample_kernel(q_ref, kn_ref, vn_ref, ck_ref, cv_ref, un_ref, cc_ref, x_ref, bias_ref, gmat_ref,
                       w_dw_ref, b_dw_ref, ln_g_ref, ln_b_ref, g_a_ref, g_c_ref, w_out_ref, g_post_ref,
                       o_ref, u_sc):
    t = q_ref.shape[0]
    past = ck_ref.shape[1]
    kb = jnp.concatenate([ck_ref[0].astype(BF16), kn_ref[...]], axis=0)
    vb = jnp.concatenate([cv_ref[0].astype(BF16), vn_ref[...]], axis=0)
    a = _band_attn_small(q_ref[...], kb, vb, bias_ref, t, past + t)

    u_sc[:CONV_WIDTH - 1, :] = cc_ref[0]
    u_sc[CONV_WIDTH - 1:, :] = un_ref[...]
    acc = jnp.zeros((t, D_CONV), F32)
    for j in range(CONV_WIDTH):
        acc = acc + u_sc[j:j + t, :] * w_dw_ref[j:j + 1, :]
    c = _conv_tail(acc, gmat_ref, b_dw_ref, ln_g_ref, ln_b_ref)
    o_ref[...] = _merge(a, c, x_ref[...], g_a_ref, g_c_ref, w_out_ref, g_post_ref)


def _mix_sample(q, kb, vb, u, x, cache_k, cache_v, cache_c, bias, gmat, w_dw, b_dw, ln_g, ln_b, g_a, g_c, w_out,
                g_post):
    n_b, past = cache_k.shape[0], cache_k.shape[1]
    t = x.shape[0] // n_b
    row = lambda n: pl.BlockSpec((t, n), lambda b: (b, 0))
    per_b = lambda a: pl.BlockSpec((1,) + a.shape[1:], lambda b: (b, 0, 0))
    vec = lambda n: _const_spec((1, n))
    return pl.pallas_call(
        _mix_sample_kernel,
        grid=(n_b,),
        in_specs=[row(D_ATTN), row(D_ATTN), row(D_ATTN), per_b(cache_k), per_b(cache_v), row(D_CONV),
                  per_b(cache_c), row(D_MODEL), _const_spec(bias.shape), _const_spec(gmat.shape),
                  _const_spec(w_dw.shape), vec(D_CONV), vec(D_CONV), vec(D_CONV), vec(D_ATTN), vec(D_CONV),
                  _const_spec(w_out.shape), vec(D_MODEL)],
        out_specs=row(D_MODEL),
        out_shape=jax.ShapeDtypeStruct(x.shape, F32),
        scratch_shapes=[pltpu.VMEM((CONV_WIDTH - 1 + t, D_CONV), F32)],
        compiler_params=_params(),
        name="mix_sample",
    )(q, kb, vb, cache_k, cache_v, u, cache_c, x, bias, gmat, w_dw, b_dw, ln_g, ln_b, g_a, g_c, w_out, g_post)


def _mem_kv_kernel(m_ref, g_ref, wk_ref, wv_ref, k_ref, v_ref):
    mn = _rms(m_ref[...], g_ref[...]).astype(BF16)
    k_ref[...] = jnp.dot(mn, wk_ref[...], preferred_element_type=F32)
    v_ref[...] = jnp.dot(mn, wv_ref[...], preferred_element_type=F32)


def _mem_kv(mem, g, wk, wv):
    rows = mem.shape[0]
    full = lambda a: _const_spec(a.shape)
    return pl.pallas_call(
        _mem_kv_kernel,
        grid=(1,),
        in_specs=[full(mem), full(g), full(wk), full(wv)],
        out_specs=[pl.BlockSpec((rows, D_MODEL), lambda i: (0, 0))] * 2,
        out_shape=[jax.ShapeDtypeStruct((rows, D_MODEL), F32)] * 2,
        compiler_params=_params(),
        name="mem_kv",
    )(mem, g, wk, wv)


def _mem_attn_kernel(x_ref, g_pre_ref, wq_ref, mk_ref, mv_ref, wo_ref, g_post_ref, o_ref):
    x = x_ref[...]
    hn = _rms(x, g_pre_ref[...]).astype(BF16)
    q = (jnp.dot(hn, wq_ref[...], preferred_element_type=F32) * (MEM_HEAD_DIM ** -0.5)).astype(BF16)
    mk = mk_ref[0].astype(BF16)
    mv = mv_ref[0].astype(BF16)
    outs = []
    for h in range(MEM_HEADS):
        sl = slice(h * MEM_HEAD_DIM, (h + 1) * MEM_HEAD_DIM)
        s = lax.dot_general(q[:, sl], mk[:, sl], (((1,), (1,)), ((), ())), preferred_element_type=F32)
        outs.append(_softmax_pv(s, mv[:, sl]).astype(BF16))
    o = jnp.concatenate(outs, axis=-1)
    y = jnp.dot(o, wo_ref[...], preferred_element_type=F32)
    o_ref[...] = x + _rms(y, g_post_ref[...])


def _mem_attn(x, g_pre, wq, mk, mv, wo, g_post, tm):
    rows = x.shape[0]
    n_tiles = rows // tm
    tiles_per_b = n_tiles // mk.shape[0]
    row = pl.BlockSpec((tm, D_MODEL), lambda i: (i, 0))
    mem = pl.BlockSpec((1, MEM_TOKENS, D_MODEL), lambda i: (i // tiles_per_b, 0, 0))
    return pl.pallas_call(
        _mem_attn_kernel,
        grid=(n_tiles,),
        in_specs=[row, _const_spec((1, D_MODEL)), _const_spec(wq.shape), mem, mem, _const_spec(wo.shape),
                  _const_spec((1, D_MODEL))],
        out_specs=row,
        out_shape=jax.ShapeDtypeStruct(x.shape, F32),
        compiler_params=_params(),
        name="mem_attn",
    )(x, g_pre, wq, mk, mv, wo, g_post)


def _mlp_kernel(x_ref, g_pre_ref, w_up_ref, w_down_ref, g_post_ref, o_ref):
    x = x_ref[...]
    hn = _rms(x, g_pre_ref[...]).astype(BF16)
    acc = jnp.zeros(x.shape, F32)
    for c in range(D_FF // FF_CHUNK):
        sl = slice(c * FF_CHUNK, (c + 1) * FF_CHUNK)
        h = jnp.maximum(jnp.dot(hn, w_up_ref[:, sl], preferred_element_type=F32), 0.0)
        acc = acc + jnp.dot((h * h).astype(BF16), w_down_ref[sl, :], preferred_element_type=F32)
    o_ref[...] = x + _rms(acc, g_post_ref[...])


def _mlp(x, g_pre, w_up, w_down, g_post, tm):
    rows = x.shape[0]
    row = pl.BlockSpec((tm, D_MODEL), lambda i: (i, 0))
    return pl.pallas_call(
        _mlp_kernel,
        grid=(rows // tm,),
        in_specs=[row, _const_spec((1, D_MODEL)), _const_spec(w_up.shape), _const_spec(w_down.shape),
                  _const_spec((1, D_MODEL))],
        out_specs=row,
        out_shape=jax.ShapeDtypeStruct(x.shape, F32),
        compiler_params=_params(),
        name="mlp",
    )(x, g_pre, w_up, w_down, g_post)


def _group_mean_matrix():
    g = D_CONV // CONV_GROUPS
    return jnp.asarray(np.kron(np.eye(CONV_GROUPS), np.full((g, g), 1.0 / g)), BF16)


def kernel(x_prompt, x_sample, mem_prompt, cache_attn_k, cache_attn_v, cache_conv, cache_mem_k, cache_mem_v, g_mix_pre, w_in, att_rel_bias, w_dw, b_dw, conv_ln_g, conv_ln_b, g_attn_out, g_conv_out, w_out, g_mix_post, g_mem_pre, g_mem_kv, w_mem_q, w_mem_k, w_mem_v, w_mem_o, g_mem_post, g_ffn_pre, w_ffn_up, w_ffn_down, g_ffn_post):
    n_bp, seq, _ = x_prompt.shape
    n_bs, t_new, _ = x_sample.shape
    depth = w_in.shape[0]
    assert n_bp == 1 and seq % ROW_TILE == 0 and ROW_TILE == BAND_PAST
    keep_s = cache_attn_k.shape[2]
    assert keep_s == BAND_PAST and t_new <= CHUNK

    xp = x_prompt.reshape(seq, D_MODEL)
    xs = x_sample.reshape(n_bs * t_new, D_MODEL)
    mem = mem_prompt.reshape(MEM_TOKENS, D_MODEL)
    gmat = _group_mean_matrix()
    vec = lambda a: a.reshape(1, -1)
    p_k, p_v, p_c, p_mk, p_mv, s_k, s_v, s_c = [], [], [], [], [], [], [], []

    for l in range(depth):
        w_in_b = w_in[l].astype(BF16)
        w_out_b = w_out[l].astype(BF16)
        wq_b, wk_b, wv_b, wo_b = (w[l].astype(BF16) for w in (w_mem_q, w_mem_k, w_mem_v, w_mem_o))
        w_up_b = w_ffn_up[l].astype(BF16)
        w_down_b = w_ffn_down[l].astype(BF16)
        bias = _rel_bias(att_rel_bias[l])
        mix_w = (bias, gmat, w_dw[l], vec(b_dw[l]), vec(conv_ln_g[l]), vec(conv_ln_b[l]), vec(g_attn_out[l]),
                 vec(g_conv_out[l]), w_out_b, vec(g_mix_post[l]))

        q, kb, vb, u, k_last, v_last = _in_proj(xp, vec(g_mix_pre[l]), w_in_b, ROW_TILE)
        xp = _mix_prompt(q, kb, vb, u, xp, *mix_w)
        p_k.append(k_last.reshape(n_bp, BAND_PAST, ATT_HEADS, ATT_HEAD_DIM))
        p_v.append(v_last.reshape(n_bp, BAND_PAST, ATT_HEADS, ATT_HEAD_DIM))
        p_c.append(u[seq - (CONV_WIDTH - 1):].reshape(n_bp, CONV_WIDTH - 1, D_CONV))

        ck = cache_attn_k[l].reshape(n_bs, keep_s, D_ATTN)
        cv = cache_attn_v[l].reshape(n_bs, keep_s, D_ATTN)
        q, kb, vb, u, k_new, v_new = _in_proj(xs, vec(g_mix_pre[l]), w_in_b, n_bs * t_new)
        xs = _mix_sample(q, kb, vb, u, xs, ck, cv, cache_conv[l], *mix_w)
        heads = lambda a: a.reshape(n_bs, t_new, ATT_HEADS, ATT_HEAD_DIM)
        s_k.append(jnp.concatenate([cache_attn_k[l], heads(k_new)], axis=1)[:, t_new:])
        s_v.append(jnp.concatenate([cache_attn_v[l], heads(v_new)], axis=1)[:, t_new:])
        s_c.append(jnp.concatenate([cache_conv[l], u.reshape(n_bs, t_new, D_CONV)], axis=1)[:, t_new:])

        mk, mv = _mem_kv(mem, vec(g_mem_kv[l]), wk_b, wv_b)
        p_mk.append(mk.reshape(n_bp, MEM_TOKENS, MEM_HEADS, MEM_HEAD_DIM))
        p_mv.append(mv.reshape(n_bp, MEM_TOKENS, MEM_HEADS, MEM_HEAD_DIM))
        mem_w = (vec(g_mem_pre[l]), wq_b)
        xp = _mem_attn(xp, *mem_w, mk[None], mv[None], wo_b, vec(g_mem_post[l]), ROW_TILE)
        xs = _mem_attn(xs, *mem_w, cache_mem_k[l].reshape(n_bs, MEM_TOKENS, D_MODEL),
                       cache_mem_v[l].reshape(n_bs, MEM_TOKENS, D_MODEL), wo_b, vec(g_mem_post[l]), t_new)

        ffn_w = (vec(g_ffn_pre[l]), w_up_b, w_down_b, vec(g_ffn_post[l]))
        xp = _mlp(xp, *ffn_w, ROW_TILE)
        xs = _mlp(xs, *ffn_w, n_bs * t_new)

    return (xp.reshape(x_prompt.shape), xs.reshape(x_sample.shape), jnp.stack(p_k), jnp.stack(p_v),
            jnp.stack(p_c), jnp.stack(p_mk), jnp.stack(p_mv), jnp.stack(s_k), jnp.stack(s_v), jnp.stack(s_c))
```

```python
import functools

import jax
import jax.numpy as jnp
import numpy as np
from jax import lax
from jax.experimental import pallas as pl
from jax.experimental.pallas import tpu as pltpu

F32 = jnp.float32
BF16 = jnp.bfloat16

D_MODEL = 1024
CHUNK = 64
LEFT_CHUNKS = 8
BAND_CHUNKS = LEFT_CHUNKS + 1
BAND_PAST = LEFT_CHUNKS * CHUNK
BAND_KEYS = BAND_CHUNKS * CHUNK
D_ATTN = 512
D_CONV = 512
ATT_HEADS = 8
ATT_HEAD_DIM = 64
REL_CLIP = 128
CONV_WIDTH = 31
CONV_GROUPS = 8
MEM_TOKENS = 256
MEM_HEADS = 4
MEM_HEAD_DIM = 256
D_FF = 4096
EPS = 1e-6
NEG_INF = -1e30
LOG2E = 1.4426950408889634

LANES = 128
ROW_TILE = 512
TAIL_ROWS = 32
FF_CHUNK = 1024
VMEM_LIMIT_BYTES = 56 * 1024 * 1024

PAIR_ROWS = 2 * CHUNK
PAIR_KEYS = BAND_KEYS + CHUNK
HEAD_PAIRS = ATT_HEADS // 2
NEAR_KEYS = 2 * REL_CLIP
FAR_KEYS = PAIR_KEYS - NEAR_KEYS


def _params(n_axes=1):
    return pltpu.CompilerParams(dimension_semantics=("arbitrary",) * n_axes,
                                vmem_limit_bytes=VMEM_LIMIT_BYTES)


def _const_spec(shape):
    zeros = (0,) * len(shape)
    return pl.BlockSpec(shape, lambda *_: zeros, pipeline_mode=pl.Buffered(1))


def _rms(x, g):
    ms = jnp.mean(x * x, axis=-1, keepdims=True)
    return x * lax.rsqrt(ms + EPS) * g


def _softmax_pv(s, v):
    m = jnp.max(s, axis=-1, keepdims=True)
    e = jnp.exp(s - m)
    l = jnp.sum(e, axis=-1, keepdims=True)
    o = jnp.dot(e.astype(BF16), v, preferred_element_type=F32)
    return o / l


def _bias_kernel(t_ref, o_ref):
    p = pl.program_id(0)
    width = 2 * NEAR_KEYS
    row = lax.broadcasted_iota(jnp.int32, (PAIR_ROWS, width), 0)
    pos = lax.broadcasted_iota(jnp.int32, (8, width), 1)
    idx = jnp.clip(REL_CLIP - (pos - NEAR_KEYS), -REL_CLIP, REL_CLIP) + REL_CLIP
    col = lax.broadcasted_iota(jnp.int32, (PAIR_ROWS, NEAR_KEYS), 1) + FAR_KEYS
    out_row = lax.broadcasted_iota(jnp.int32, (PAIR_ROWS, NEAR_KEYS), 0)
    band_lo = (out_row // CHUNK) * CHUNK
    in_band = jnp.logical_and(col >= band_lo, col < band_lo + BAND_KEYS)
    for half in range(2):
        h = 2 * p + half
        far = t_ref[h, 2 * REL_CLIP]

        def body(i, acc):
            return jnp.where(idx == i, t_ref[h, i] - far, acc)

        base = lax.fori_loop(1, 2 * REL_CLIP + 1, body, jnp.zeros((8, width), F32))
        base = jnp.where(pos >= NEAR_KEYS, base, 0.0)
        x = jnp.concatenate([base] * (PAIR_ROWS // 8), axis=0)
        for bit in range(PAIR_ROWS.bit_length() - 1):
            x = jnp.where(((row >> bit) & 1) == 1, pltpu.roll(x, 1 << bit, axis=1), x)
        tile = x[:, NEAR_KEYS:] * LOG2E
        o_ref[0, half * PAIR_ROWS:(half + 1) * PAIR_ROWS, :] = jnp.where(in_band, tile, NEG_INF)


def _rel_bias(table):
    return pl.pallas_call(
        _bias_kernel,
        grid=(HEAD_PAIRS,),
        in_specs=[pl.BlockSpec(memory_space=pltpu.SMEM)],
        out_specs=pl.BlockSpec((1, 2 * PAIR_ROWS, NEAR_KEYS), lambda p: (p, 0, 0)),
        out_shape=jax.ShapeDtypeStruct((HEAD_PAIRS, 2 * PAIR_ROWS, NEAR_KEYS), F32),
        compiler_params=_params(),
        name="rel_bias",
    )(table)


def _project(x_ref, g_ref, w_ref, q_ref, kb_ref, vb_ref, kt_ref, vt_ref, on_u):
    xn = _rms(x_ref[...], g_ref[...]).astype(BF16)

    def proj(c):
        return jnp.dot(xn, w_ref[:, c * D_ATTN:(c + 1) * D_ATTN], preferred_element_type=F32)

    fill = on_u(proj(3) * jax.nn.sigmoid(proj(4)))
    q_ref[...] = (proj(0) * (ATT_HEAD_DIM ** -0.5 * LOG2E)).astype(BF16)
    fill[0]()
    k = proj(1)
    kb_ref[...] = k.astype(BF16)
    kt_ref[...] = k
    fill[1]()
    v = proj(2)
    vb_ref[...] = v.astype(BF16)
    vt_ref[...] = v
    fill[2]()


def _in_proj_kernel(x_ref, g_ref, w_ref, q_ref, kb_ref, vb_ref, u_ref, kt_ref, vt_ref):
    def on_u(u):
        u_ref[...] = u
        return [lambda: None] * 3

    _project(x_ref, g_ref, w_ref, q_ref, kb_ref, vb_ref, kt_ref, vt_ref, on_u)


def _in_proj_conv_kernel(x_ref, g_ref, w_ref, w_dw_ref, q_ref, kb_ref, vb_ref, c_ref, kt_ref, vt_ref, ut_ref,
                         u_sc):
    i = pl.program_id(0)
    tm = x_ref.shape[0]

    @pl.when(i == 0)
    def _():
        u_sc[:, :TAIL_ROWS, :] = jnp.zeros((D_CONV // LANES, TAIL_ROWS, LANES), F32)

    @pl.when(i > 0)
    def _():
        u_sc[:, :TAIL_ROWS, :] = u_sc[:, tm:, :]

    off = TAIL_ROWS - (CONV_WIDTH - 1)

    def conv_rows(r0):
        for cb in range(D_CONV // LANES):
            ls = slice(cb * LANES, (cb + 1) * LANES)
            win = {}
            acc = [None] * (CHUNK // 8)
            for j in range(CONV_WIDTH):
                wj = w_dw_ref[j:j + 1, ls]
                for m in range(CHUNK // 8):
                    s = r0 + off + j + 8 * m
                    if s not in win:
                        win[s] = u_sc[cb, s:s + 8, :]
                    term = win[s] * wj
                    acc[m] = term if acc[m] is None else acc[m] + term
            for m in range(CHUNK // 8):
                c_ref[r0 + 8 * m:r0 + 8 * m + 8, ls] = acc[m]

    def on_u(u):
        for cb in range(D_CONV // LANES):
            u_sc[cb, TAIL_ROWS:, :] = u[:, cb * LANES:(cb + 1) * LANES]
        ut_ref[...] = u[tm - TAIL_ROWS:, :]
        starts = list(range(0, tm, CHUNK))
        thirds = [starts[n::3] for n in range(3)]
        return [lambda rows=rows: [conv_rows(r0) for r0 in rows] for rows in thirds]

    _project(x_ref, g_ref, w_ref, q_ref, kb_ref, vb_ref, kt_ref, vt_ref, on_u)


def _in_proj(x, g, w, w_dw, tm):
    rows = x.shape[0]
    n_out = w.shape[1]
    row = lambda n: pl.BlockSpec((tm, n), lambda i: (i, 0))
    last = lambda r: pl.BlockSpec((r, D_ATTN), lambda i: (0, 0))
    in_specs = [row(D_MODEL), _const_spec((1, D_MODEL)), _const_spec((D_MODEL, n_out))]
    out_specs = [row(D_ATTN), row(D_ATTN), row(D_ATTN), row(D_CONV), last(tm), last(tm)]
    out_shape = [jax.ShapeDtypeStruct((rows, D_ATTN), BF16),
                 jax.ShapeDtypeStruct((rows, D_ATTN), BF16),
                 jax.ShapeDtypeStruct((rows, D_ATTN), BF16),
                 jax.ShapeDtypeStruct((rows, D_CONV), F32),
                 jax.ShapeDtypeStruct((tm, D_ATTN), F32),
                 jax.ShapeDtypeStruct((tm, D_ATTN), F32)]
    if w_dw is None:
        return pl.pallas_call(
            _in_proj_kernel, grid=(rows // tm,), in_specs=in_specs, out_specs=out_specs, out_shape=out_shape,
            compiler_params=_params(), name="in_proj",
        )(x, g, w)
    return pl.pallas_call(
        _in_proj_conv_kernel,
        grid=(rows // tm,),
        in_specs=in_specs + [_const_spec(w_dw.shape)],
        out_specs=out_specs + [last(TAIL_ROWS)],
        out_shape=out_shape + [jax.ShapeDtypeStruct((TAIL_ROWS, D_CONV), F32)],
        scratch_shapes=[pltpu.VMEM((D_CONV // LANES, TAIL_ROWS + tm, LANES), F32)],
        compiler_params=_params(),
        name="in_proj_conv",
    )(x, g, w, w_dw)


_NT = (((1,), (1,)), ((), ()))


def _pair_scores(q2, kw, bias_near, key_lo):
    lane = lax.broadcasted_iota(jnp.int32, q2.shape, 1)
    zero = jnp.zeros_like(q2)
    lhs = jnp.concatenate([jnp.where(lane < ATT_HEAD_DIM, q2, zero), jnp.where(lane < ATT_HEAD_DIM, zero, q2)],
                          axis=0)
    s_far = lax.dot_general(lhs, kw[:FAR_KEYS], _NT, preferred_element_type=F32)
    s_near = lax.dot_general(lhs, kw[FAR_KEYS:], _NT, preferred_element_type=F32) + bias_near
    row = lax.broadcasted_iota(jnp.int32, (2 * PAIR_ROWS, LANES), 0)
    col = lax.broadcasted_iota(jnp.int32, (2 * PAIR_ROWS, LANES), 1)
    s_0 = jnp.where(col >= ((row // CHUNK) % 2) * CHUNK, s_far[:, :LANES], NEG_INF)
    pieces = [s_0, s_far[:, LANES:], s_near]
    if key_lo is not None:
        start = 0
        for n, s in enumerate(pieces):
            c = lax.broadcasted_iota(jnp.int32, s.shape, 1) + start
            pieces[n] = jnp.where(c >= key_lo, s, NEG_INF)
            start += s.shape[1]
    return pieces


def _pair_softmax_pv(pieces, vw):
    blocks = [s[:, c:c + LANES] for s in pieces for c in range(0, s.shape[1], LANES)]
    m = jnp.max(functools.reduce(jnp.maximum, blocks), axis=-1, keepdims=True)
    es = [jnp.exp2(b - m) for b in blocks]
    l = jnp.sum(functools.reduce(jnp.add, es), axis=-1, keepdims=True)
    e = jnp.concatenate([b.astype(BF16) for b in es], axis=-1)
    o = jnp.dot(e, vw, preferred_element_type=F32) / l
    out_lane = lax.broadcasted_iota(jnp.int32, (PAIR_ROWS, LANES), 1)
    return jnp.where(out_lane < ATT_HEAD_DIM, o[:PAIR_ROWS], o[PAIR_ROWS:])


def _band_attn_small(q, kb, vb, bias_ref, n_q, n_k):
    outs = []
    for h in range(ATT_HEADS):
        sl = slice(h * ATT_HEAD_DIM, (h + 1) * ATT_HEAD_DIM)
        r0 = (h % 2) * PAIR_ROWS
        s_far = lax.dot_general(q[:, sl], kb[:FAR_KEYS, sl], _NT, preferred_element_type=F32)
        s_near = lax.dot_general(q[:, sl], kb[FAR_KEYS:, sl], _NT, preferred_element_type=F32)
        s_near = s_near + bias_ref[h // 2, r0:r0 + n_q, :n_k - FAR_KEYS]
        m = jnp.maximum(jnp.max(s_far, axis=-1, keepdims=True), jnp.max(s_near, axis=-1, keepdims=True))
        e_far = jnp.exp2(s_far - m)
        e_near = jnp.exp2(s_near - m)
        l = jnp.sum(e_far, axis=-1, keepdims=True) + jnp.sum(e_near, axis=-1, keepdims=True)
        o = (jnp.dot(e_far.astype(BF16), vb[:FAR_KEYS, sl], preferred_element_type=F32)
             + jnp.dot(e_near.astype(BF16), vb[FAR_KEYS:, sl], preferred_element_type=F32))
        outs.append(o / l)
    return jnp.concatenate(outs, axis=-1)


def _conv_tail(c, gmat_ref, b_dw_ref, ln_g_ref, ln_b_ref):
    c = c + b_dw_ref[...]
    gm = gmat_ref[...]
    c_hi = c.astype(BF16)
    c_lo = (c - c_hi.astype(F32)).astype(BF16)
    mu = jnp.dot(c_hi, gm, preferred_element_type=F32) + jnp.dot(c_lo, gm, preferred_element_type=F32)
    d = c - mu
    var = jnp.dot((d * d).astype(BF16), gm, preferred_element_type=F32)
    y = d * lax.rsqrt(var + EPS) * ln_g_ref[...] + ln_b_ref[...]
    return y * jax.nn.sigmoid(y)


def _merge(a, c, x, g_a_ref, g_c_ref, w_out_ref, g_post_ref):
    an = _rms(a, g_a_ref[...]).astype(BF16)
    cn = _rms(c, g_c_ref[...]).astype(BF16)
    y = (jnp.dot(an, w_out_ref[:D_ATTN, :], preferred_element_type=F32)
         + jnp.dot(cn, w_out_ref[D_ATTN:, :], preferred_element_type=F32))
    return x + _rms(y, g_post_ref[...])


def _mix_prompt_kernel(q_ref, kp_ref, kc_ref, vp_ref, vc_ref, c_ref, x_ref, bias_ref, gmat_ref,
                       b_dw_ref, ln_g_ref, ln_b_ref, g_a_ref, g_c_ref, w_out_ref, g_post_ref,
                       o_ref, k_sc, v_sc, a_sc):
    i = pl.program_id(0)

    k_sc[:ROW_TILE, :] = kp_ref[...]
    k_sc[ROW_TILE:, :] = kc_ref[...]
    v_sc[:ROW_TILE, :] = vp_ref[...]
    v_sc[ROW_TILE:, :] = vc_ref[...]

    def scores(r0, p, key_lo):
        ls = slice(p * LANES, (p + 1) * LANES)
        return _pair_scores(q_ref[pl.ds(r0, PAIR_ROWS), ls], k_sc[pl.ds(r0, PAIR_KEYS), ls], bias_ref[p], key_lo)

    def finish(r0, p, pieces):
        ls = slice(p * LANES, (p + 1) * LANES)
        a_sc[pl.ds(r0, PAIR_ROWS), ls] = _pair_softmax_pv(pieces, v_sc[pl.ds(r0, PAIR_KEYS), ls])

    @pl.when(i == 0)
    def _():
        def pair_body(a, carry):
            r0 = pl.multiple_of(a * PAIR_ROWS, PAIR_ROWS)
            for p in range(HEAD_PAIRS):
                finish(r0, p, scores(r0, p, ROW_TILE - r0))
            return carry

        lax.fori_loop(0, ROW_TILE // PAIR_ROWS, pair_body, 0)

    @pl.when(i > 0)
    def _():
        units = [(r0, p) for r0 in range(0, ROW_TILE, PAIR_ROWS) for p in range(HEAD_PAIRS)]
        pieces = scores(*units[0], None)
        for n, (r0, p) in enumerate(units):
            ahead = scores(*units[n + 1], None) if n + 1 < len(units) else None
            finish(r0, p, pieces)
            pieces = ahead

    c = _conv_tail(c_ref[...], gmat_ref, b_dw_ref, ln_g_ref, ln_b_ref)
    o_ref[...] = _merge(a_sc[...], c, x_ref[...], g_a_ref, g_c_ref, w_out_ref, g_post_ref)


def _mix_prompt(q, kb, vb, c, x, bias, gmat, b_dw, ln_g, ln_b, g_a, g_c, w_out, g_post):
    rows = x.shape[0]
    cur = lambda n: pl.BlockSpec((ROW_TILE, n), lambda i: (i, 0))
    prev = lambda n: pl.BlockSpec((ROW_TILE, n), lambda i: (jnp.maximum(i - 1, 0), 0))
    vec = lambda n: _const_spec((1, n))
    return pl.pallas_call(
        _mix_prompt_kernel,
        grid=(rows // ROW_TILE,),
        in_specs=[cur(D_ATTN), prev(D_ATTN), cur(D_ATTN), prev(D_ATTN), cur(D_ATTN), cur(D_CONV),
                  cur(D_MODEL), _const_spec(bias.shape), _const_spec(gmat.shape),
                  vec(D_CONV), vec(D_CONV), vec(D_CONV), vec(D_ATTN), vec(D_CONV),
                  _const_spec(w_out.shape), vec(D_MODEL)],
        out_specs=cur(D_MODEL),
        out_shape=jax.ShapeDtypeStruct((rows, D_MODEL), F32),
        scratch_shapes=[pltpu.VMEM((2 * ROW_TILE, D_ATTN), BF16), pltpu.VMEM((2 * ROW_TILE, D_ATTN), BF16),
                        pltpu.VMEM((ROW_TILE, D_ATTN), F32)],
        compiler_params=_params(),
        name="mix_prompt",
    )(q, kb, kb, vb, vb, c, x, bias, gmat, b_dw, ln_g, ln_b, g_a, g_c, w_out, g_post)


def _mix_sample_kernel(q_ref, kn_ref, vn_ref, ck_ref, cv_ref, un_ref, cc_ref, x_ref, bias_ref, gmat_ref,
                       w_dw_ref, b_dw_ref, ln_g_ref, ln_b_ref, g_a_ref, g_c_ref, w_out_ref, g_post_ref,
                       o_ref, u_sc):
    t = q_ref.shape[0]
    past = ck_ref.shape[1]
    kb = jnp.concatenate([ck_ref[0].astype(BF16), kn_ref[...]], axis=0)
    vb = jnp.concatenate([cv_ref[0].astype(BF16), vn_ref[...]], axis=0)
    a = _band_attn_small(q_ref[...], kb, vb, bias_ref, t, past + t)

    u_sc[:CONV_WIDTH - 1, :] = cc_ref[0]
    u_sc[CONV_WIDTH - 1:, :] = un_ref[...]
    acc = jnp.zeros((t, D_CONV), F32)
    for j in range(CONV_WIDTH):
        acc = acc + u_sc[j:j + t, :] * w_dw_ref[j:j + 1, :]
    c = _conv_tail(acc, gmat_ref, b_dw_ref, ln_g_ref, ln_b_ref)
    o_ref[...] = _merge(a, c, x_ref[...], g_a_ref, g_c_ref, w_out_ref, g_post_ref)


def _mix_sample(q, kb, vb, u, x, cache_k, cache_v, cache_c, bias, gmat, w_dw, b_dw, ln_g, ln_b, g_a, g_c, w_out,
                g_post):
    n_b, past = cache_k.shape[0], cache_k.shape[1]
    t = x.shape[0] // n_b
    row = lambda n: pl.BlockSpec((t, n), lambda b: (b, 0))
    per_b = lambda a: pl.BlockSpec((1,) + a.shape[1:], lambda b: (b, 0, 0))
    vec = lambda n: _const_spec((1, n))
    return pl.pallas_call(
        _mix_sample_kernel,
        grid=(n_b,),
        in_specs=[row(D_ATTN), row(D_ATTN), row(D_ATTN), per_b(cache_k), per_b(cache_v), row(D_CONV),
                  per_b(cache_c), row(D_MODEL), _const_spec(bias.shape), _const_spec(gmat.shape),
                  _const_spec(w_dw.shape), vec(D_CONV), vec(D_CONV), vec(D_CONV), vec(D_ATTN), vec(D_CONV),
                  _const_spec(w_out.shape), vec(D_MODEL)],
        out_specs=row(D_MODEL),
        out_shape=jax.ShapeDtypeStruct(x.shape, F32),
        scratch_shapes=[pltpu.VMEM((CONV_WIDTH - 1 + t, D_CONV), F32)],
        compiler_params=_params(),
        name="mix_sample",
    )(q, kb, vb, cache_k, cache_v, u, cache_c, x, bias, gmat, w_dw, b_dw, ln_g, ln_b, g_a, g_c, w_out, g_post)


def _mem_kv_kernel(m_ref, g_ref, wk_ref, wv_ref, k_ref, v_ref):
    mn = _rms(m_ref[...], g_ref[...]).astype(BF16)
    k_ref[...] = jnp.dot(mn, wk_ref[...], preferred_element_type=F32)
    v_ref[...] = jnp.dot(mn, wv_ref[...], preferred_element_type=F32)


def _mem_kv(mem, g, wk, wv):
    rows = mem.shape[0]
    full = lambda a: _const_spec(a.shape)
    return pl.pallas_call(
        _mem_kv_kernel,
        grid=(1,),
        in_specs=[full(mem), full(g), full(wk), full(wv)],
        out_specs=[pl.BlockSpec((rows, D_MODEL), lambda i: (0, 0))] * 2,
        out_shape=[jax.ShapeDtypeStruct((rows, D_MODEL), F32)] * 2,
        compiler_params=_params(),
        name="mem_kv",
    )(mem, g, wk, wv)


def _mem_attn_kernel(x_ref, g_pre_ref, wq_ref, mk_ref, mv_ref, wo_ref, g_post_ref, o_ref):
    x = x_ref[...]
    hn = _rms(x, g_pre_ref[...]).astype(BF16)
    q = (jnp.dot(hn, wq_ref[...], preferred_element_type=F32) * (MEM_HEAD_DIM ** -0.5)).astype(BF16)
    mk = mk_ref[0].astype(BF16)
    mv = mv_ref[0].astype(BF16)
    outs = []
    for h in range(MEM_HEADS):
        sl = slice(h * MEM_HEAD_DIM, (h + 1) * MEM_HEAD_DIM)
        s = lax.dot_general(q[:, sl], mk[:, sl], (((1,), (1,)), ((), ())), preferred_element_type=F32)
        outs.append(_softmax_pv(s, mv[:, sl]).astype(BF16))
    o = jnp.concatenate(outs, axis=-1)
    y = jnp.dot(o, wo_ref[...], preferred_element_type=F32)
    o_ref[...] = x + _rms(y, g_post_ref[...])


def _mem_attn(x, g_pre, wq, mk, mv, wo, g_post, tm):
    rows = x.shape[0]
    n_tiles = rows // tm
    tiles_per_b = n_tiles // mk.shape[0]
    row = pl.BlockSpec((tm, D_MODEL), lambda i: (i, 0))
    mem = pl.BlockSpec((1, MEM_TOKENS, D_MODEL), lambda i: (i // tiles_per_b, 0, 0))
    return pl.pallas_call(
        _mem_attn_kernel,
        grid=(n_tiles,),
        in_specs=[row, _const_spec((1, D_MODEL)), _const_spec(wq.shape), mem, mem, _const_spec(wo.shape),
                  _const_spec((1, D_MODEL))],
        out_specs=row,
        out_shape=jax.ShapeDtypeStruct(x.shape, F32),
        compiler_params=_params(),
        name="mem_attn",
    )(x, g_pre, wq, mk, mv, wo, g_post)


def _mlp_kernel(x_ref, g_pre_ref, w_up_ref, w_down_ref, g_post_ref, o_ref):
    x = x_ref[...]
    hn = _rms(x, g_pre_ref[...]).astype(BF16)
    acc = jnp.zeros(x.shape, F32)
    for c in range(D_FF // FF_CHUNK):
        sl = slice(c * FF_CHUNK, (c + 1) * FF_CHUNK)
        h = jnp.maximum(jnp.dot(hn, w_up_ref[:, sl], preferred_element_type=F32), 0.0)
        acc = acc + jnp.dot((h * h).astype(BF16), w_down_ref[sl, :], preferred_element_type=F32)
    o_ref[...] = x + _rms(acc, g_post_ref[...])


def _mlp(x, g_pre, w_up, w_down, g_post, tm):
    rows = x.shape[0]
    row = pl.BlockSpec((tm, D_MODEL), lambda i: (i, 0))
    return pl.pallas_call(
        _mlp_kernel,
        grid=(rows // tm,),
        in_specs=[row, _const_spec((1, D_MODEL)), _const_spec(w_up.shape), _const_spec(w_down.shape),
                  _const_spec((1, D_MODEL))],
        out_specs=row,
        out_shape=jax.ShapeDtypeStruct(x.shape, F32),
        compiler_params=_params(),
        name="mlp",
    )(x, g_pre, w_up, w_down, g_post)


def _group_mean_matrix():
    g = D_CONV // CONV_GROUPS
    return jnp.asarray(np.kron(np.eye(CONV_GROUPS), np.full((g, g), 1.0 / g)), BF16)


def kernel(x_prompt, x_sample, mem_prompt, cache_attn_k, cache_attn_v, cache_conv, cache_mem_k, cache_mem_v, g_mix_pre, w_in, att_rel_bias, w_dw, b_dw, conv_ln_g, conv_ln_b, g_attn_out, g_conv_out, w_out, g_mix_post, g_mem_pre, g_mem_kv, w_mem_q, w_mem_k, w_mem_v, w_mem_o, g_mem_post, g_ffn_pre, w_ffn_up, w_ffn_down, g_ffn_post):
    n_bp, seq, _ = x_prompt.shape
    n_bs, t_new, _ = x_sample.shape
    depth = w_in.shape[0]
    assert n_bp == 1 and seq % ROW_TILE == 0 and ROW_TILE == BAND_PAST
    keep_s = cache_attn_k.shape[2]
    assert keep_s == BAND_PAST and t_new <= CHUNK

    xp = x_prompt.reshape(seq, D_MODEL)
    xs = x_sample.reshape(n_bs * t_new, D_MODEL)
    mem = mem_prompt.reshape(MEM_TOKENS, D_MODEL)
    gmat = _group_mean_matrix()
    vec = lambda a: a.reshape(1, -1)
    p_k, p_v, p_c, p_mk, p_mv, s_k, s_v, s_c = [], [], [], [], [], [], [], []

    for l in range(depth):
        w_in_b = w_in[l].astype(BF16)
        w_out_b = w_out[l].astype(BF16)
        wq_b, wk_b, wv_b, wo_b = (w[l].astype(BF16) for w in (w_mem_q, w_mem_k, w_mem_v, w_mem_o))
        w_up_b = w_ffn_up[l].astype(BF16)
        w_down_b = w_ffn_down[l].astype(BF16)
        bias = _rel_bias(att_rel_bias[l])
        conv_w = (vec(b_dw[l]), vec(conv_ln_g[l]), vec(conv_ln_b[l]), vec(g_attn_out[l]), vec(g_conv_out[l]),
                  w_out_b, vec(g_mix_post[l]))

        q, kb, vb, c, k_last, v_last, u_last = _in_proj(xp, vec(g_mix_pre[l]), w_in_b, w_dw[l], ROW_TILE)
        xp = _mix_prompt(q, kb, vb, c, xp, bias, gmat, *conv_w)
        p_k.append(k_last.reshape(n_bp, BAND_PAST, ATT_HEADS, ATT_HEAD_DIM))
        p_v.append(v_last.reshape(n_bp, BAND_PAST, ATT_HEADS, ATT_HEAD_DIM))
        p_c.append(u_last[TAIL_ROWS - (CONV_WIDTH - 1):].reshape(n_bp, CONV_WIDTH - 1, D_CONV))

        ck = cache_attn_k[l].reshape(n_bs, keep_s, D_ATTN)
        cv = cache_attn_v[l].reshape(n_bs, keep_s, D_ATTN)
        q, kb, vb, u, k_new, v_new = _in_proj(xs, vec(g_mix_pre[l]), w_in_b, None, n_bs * t_new)
        xs = _mix_sample(q, kb, vb, u, xs, ck, cv, cache_conv[l], bias, gmat, w_dw[l], *conv_w)
        heads = lambda a: a.reshape(n_bs, t_new, ATT_HEADS, ATT_HEAD_DIM)
        s_k.append(jnp.concatenate([cache_attn_k[l], heads(k_new)], axis=1)[:, t_new:])
        s_v.append(jnp.concatenate([cache_attn_v[l], heads(v_new)], axis=1)[:, t_new:])
        s_c.append(jnp.concatenate([cache_conv[l], u.reshape(n_bs, t_new, D_CONV)], axis=1)[:, t_new:])

        mk, mv = _mem_kv(mem, vec(g_mem_kv[l]), wk_b, wv_b)
        p_mk.append(mk.reshape(n_bp, MEM_TOKENS, MEM_HEADS, MEM_HEAD_DIM))
        p_mv.append(mv.reshape(n_bp, MEM_TOKENS, MEM_HEADS, MEM_HEAD_DIM))
        mem_w = (vec(g_mem_pre[l]), wq_b)
        xp = _mem_attn(xp, *mem_w, mk[None], mv[None], wo_b, vec(g_mem_post[l]), ROW_TILE)
        xs = _mem_attn(xs, *mem_w, cache_mem_k[l].reshape(n_bs, MEM_TOKENS, D_MODEL),
                       cache_mem_v[l].reshape(n_bs, MEM_TOKENS, D_MODEL), wo_b, vec(g_mem_post[l]), t_new)

        ffn_w = (vec(g_ffn_pre[l]), w_up_b, w_down_b, vec(g_ffn_post[l]))
        xp = _mlp(xp, *ffn_w, ROW_TILE)
        xs = _mlp(xs, *ffn_w, n_bs * t_new)

    return (xp.reshape(x_prompt.shape), xs.reshape(x_sample.shape), jnp.stack(p_k), jnp.stack(p_v),
            jnp.stack(p_c), jnp.stack(p_mk), jnp.stack(p_mv), jnp.stack(s_k), jnp.stack(s_v), jnp.stack(s_c))
```

```python
import functools

import jax
import jax.numpy as jnp
import numpy as np
from jax import lax
from jax.experimental import pallas as pl
from jax.experimental.pallas import tpu as pltpu

F32 = jnp.float32
BF16 = jnp.bfloat16

D_MODEL = 1024
CHUNK = 64
LEFT_CHUNKS = 8
BAND_CHUNKS = LEFT_CHUNKS + 1
BAND_PAST = LEFT_CHUNKS * CHUNK
BAND_KEYS = BAND_CHUNKS * CHUNK
D_ATTN = 512
D_CONV = 512
ATT_HEADS = 8
ATT_HEAD_DIM = 64
REL_CLIP = 128
CONV_WIDTH = 31
CONV_GROUPS = 8
MEM_TOKENS = 256
MEM_HEADS = 4
MEM_HEAD_DIM = 256
D_FF = 4096
EPS = 1e-6
NEG_INF = -1e30
LOG2E = 1.4426950408889634

LANES = 128
ROW_TILE = 512
TAIL_ROWS = 32
FF_CHUNK = 1024
VMEM_LIMIT_BYTES = 56 * 1024 * 1024

PAIR_ROWS = 2 * CHUNK
PAIR_KEYS = BAND_KEYS + CHUNK
HEAD_PAIRS = ATT_HEADS // 2
NEAR_KEYS = 2 * REL_CLIP
FAR_KEYS = PAIR_KEYS - NEAR_KEYS


def _params(n_axes=1):
    return pltpu.CompilerParams(dimension_semantics=("arbitrary",) * n_axes,
                                vmem_limit_bytes=VMEM_LIMIT_BYTES)


def _const_spec(shape):
    zeros = (0,) * len(shape)
    return pl.BlockSpec(shape, lambda *_: zeros, pipeline_mode=pl.Buffered(1))


def _rms(x, g):
    ms = jnp.mean(x * x, axis=-1, keepdims=True)
    return x * lax.rsqrt(ms + EPS) * g


def _softmax_pv(s, v):
    m = jnp.max(s, axis=-1, keepdims=True)
    e = jnp.exp(s - m)
    l = jnp.sum(e, axis=-1, keepdims=True)
    o = jnp.dot(e.astype(BF16), v, preferred_element_type=F32)
    return o / l


def _bias_kernel(t_ref, o_ref):
    width = 2 * NEAR_KEYS
    t = t_ref[...]
    t_hi = t.astype(BF16)
    rest = t - t_hi.astype(F32)
    t_mid = rest.astype(BF16)
    t_lo = (rest - t_mid.astype(F32)).astype(BF16)
    src = lax.broadcasted_iota(jnp.int32, (t.shape[1], width), 0)
    dst = lax.broadcasted_iota(jnp.int32, (t.shape[1], width), 1)
    pick = jnp.where(jnp.logical_and(src == width - dst, dst >= NEAR_KEYS), 1.0, 0.0).astype(BF16)
    base = (jnp.dot(t_hi, pick, preferred_element_type=F32) + jnp.dot(t_mid, pick, preferred_element_type=F32)
            + jnp.dot(t_lo, pick, preferred_element_type=F32))
    pos = lax.broadcasted_iota(jnp.int32, base.shape, 1)
    base = jnp.where(pos >= NEAR_KEYS, base - t[:, 2 * REL_CLIP:2 * REL_CLIP + 1], 0.0)

    row = lax.broadcasted_iota(jnp.int32, (PAIR_ROWS, width), 0)
    col = lax.broadcasted_iota(jnp.int32, (PAIR_ROWS, NEAR_KEYS), 1) + FAR_KEYS
    out_row = lax.broadcasted_iota(jnp.int32, (PAIR_ROWS, NEAR_KEYS), 0)
    band_lo = (out_row // CHUNK) * CHUNK
    in_band = jnp.logical_and(col >= band_lo, col < band_lo + BAND_KEYS)
    for h in range(ATT_HEADS):
        x = jnp.broadcast_to(base[h:h + 1, :], (PAIR_ROWS, width))
        for bit in range(PAIR_ROWS.bit_length() - 1):
            x = jnp.where(((row >> bit) & 1) == 1, pltpu.roll(x, 1 << bit, axis=1), x)
        tile = x[:, NEAR_KEYS:] * LOG2E
        o_ref[h // 2, (h % 2) * PAIR_ROWS:(h % 2 + 1) * PAIR_ROWS, :] = jnp.where(in_band, tile, NEG_INF)


def _rel_bias(table):
    pad = FAR_KEYS - table.shape[1]
    table = jnp.pad(table, ((0, 0), (0, pad)))
    shape = (HEAD_PAIRS, 2 * PAIR_ROWS, NEAR_KEYS)
    return pl.pallas_call(
        _bias_kernel,
        grid=(1,),
        in_specs=[_const_spec(table.shape)],
        out_specs=pl.BlockSpec(shape, lambda i: (0, 0, 0)),
        out_shape=jax.ShapeDtypeStruct(shape, F32),
        compiler_params=_params(),
        name="rel_bias",
    )(table)


def _project(x_ref, g_ref, w_ref, q_ref, kb_ref, vb_ref, kt_ref, vt_ref, on_u):
    xn = _rms(x_ref[...], g_ref[...]).astype(BF16)

    def proj(c):
        return jnp.dot(xn, w_ref[:, c * D_ATTN:(c + 1) * D_ATTN], preferred_element_type=F32)

    fill = on_u(proj(3) * jax.nn.sigmoid(proj(4)))
    q_ref[...] = (proj(0) * (ATT_HEAD_DIM ** -0.5 * LOG2E)).astype(BF16)
    fill[0]()
    k = proj(1)
    kb_ref[...] = k.astype(BF16)
    kt_ref[...] = k
    fill[1]()
    v = proj(2)
    vb_ref[...] = v.astype(BF16)
    vt_ref[...] = v
    fill[2]()


def _in_proj_kernel(x_ref, g_ref, w_ref, q_ref, kb_ref, vb_ref, u_ref, kt_ref, vt_ref):
    def on_u(u):
        u_ref[...] = u
        return [lambda: None] * 3

    _project(x_ref, g_ref, w_ref, q_ref, kb_ref, vb_ref, kt_ref, vt_ref, on_u)


def _cast_specs(weights, steps):
    specs = [pl.BlockSpec((w.shape[0] // steps, w.shape[1]), lambda i: (i, 0)) for w in weights]
    return specs, list(specs), [jax.ShapeDtypeStruct(w.shape, BF16) for w in weights]


def _cast_slices(f32_refs, bf16_refs):
    for src, dst in zip(f32_refs, bf16_refs, strict=True):
        dst[...] = src[...].astype(BF16)


def _in_proj_conv_kernel(n_cast, x_ref, g_ref, w_ref, w_dw_ref, *refs):
    q_ref, kb_ref, vb_ref, c_ref, kt_ref, vt_ref, ut_ref = refs[n_cast:n_cast + 7]
    u_sc = refs[-1]
    _cast_slices(refs[:n_cast], refs[n_cast + 7:-1])
    i = pl.program_id(0)
    tm = x_ref.shape[0]

    @pl.when(i == 0)
    def _():
        u_sc[:, :TAIL_ROWS, :] = jnp.zeros((D_CONV // LANES, TAIL_ROWS, LANES), F32)

    @pl.when(i > 0)
    def _():
        u_sc[:, :TAIL_ROWS, :] = u_sc[:, tm:, :]

    off = TAIL_ROWS - (CONV_WIDTH - 1)

    def conv_rows(r0):
        for cb in range(D_CONV // LANES):
            ls = slice(cb * LANES, (cb + 1) * LANES)
            win = {}
            acc = [None] * (CHUNK // 8)
            for j in range(CONV_WIDTH):
                wj = w_dw_ref[j:j + 1, ls]
                for m in range(CHUNK // 8):
                    s = r0 + off + j + 8 * m
                    if s not in win:
                        win[s] = u_sc[cb, s:s + 8, :]
                    term = win[s] * wj
                    acc[m] = term if acc[m] is None else acc[m] + term
            for m in range(CHUNK // 8):
                c_ref[r0 + 8 * m:r0 + 8 * m + 8, ls] = acc[m]

    def on_u(u):
        for cb in range(D_CONV // LANES):
            u_sc[cb, TAIL_ROWS:, :] = u[:, cb * LANES:(cb + 1) * LANES]
        ut_ref[...] = u[tm - TAIL_ROWS:, :]
        starts = list(range(0, tm, CHUNK))
        thirds = [starts[n::3] for n in range(3)]
        return [lambda rows=rows: [conv_rows(r0) for r0 in rows] for rows in thirds]

    _project(x_ref, g_ref, w_ref, q_ref, kb_ref, vb_ref, kt_ref, vt_ref, on_u)


def _in_proj(x, g, w, w_dw, tm, to_cast=()):
    rows = x.shape[0]
    n_out = w.shape[1]
    row = lambda n: pl.BlockSpec((tm, n), lambda i: (i, 0))
    last = lambda r: pl.BlockSpec((r, D_ATTN), lambda i: (0, 0))
    in_specs = [row(D_MODEL), _const_spec((1, D_MODEL)), _const_spec((D_MODEL, n_out))]
    out_specs = [row(D_ATTN), row(D_ATTN), row(D_ATTN), row(D_CONV), last(tm), last(tm)]
    out_shape = [jax.ShapeDtypeStruct((rows, D_ATTN), BF16),
                 jax.ShapeDtypeStruct((rows, D_ATTN), BF16),
                 jax.ShapeDtypeStruct((rows, D_ATTN), BF16),
                 jax.ShapeDtypeStruct((rows, D_CONV), F32),
                 jax.ShapeDtypeStruct((tm, D_ATTN), F32),
                 jax.ShapeDtypeStruct((tm, D_ATTN), F32)]
    if w_dw is None:
        return pl.pallas_call(
            _in_proj_kernel, grid=(rows // tm,), in_specs=in_specs, out_specs=out_specs, out_shape=out_shape,
            compiler_params=_params(), name="in_proj",
        )(x, g, w)
    cast_in, cast_out, cast_shape = _cast_specs(to_cast, rows // tm)
    return pl.pallas_call(
        functools.partial(_in_proj_conv_kernel, len(to_cast)),
        grid=(rows // tm,),
        in_specs=in_specs + [_const_spec(w_dw.shape)] + cast_in,
        out_specs=out_specs + [last(TAIL_ROWS)] + cast_out,
        out_shape=out_shape + [jax.ShapeDtypeStruct((TAIL_ROWS, D_CONV), F32)] + cast_shape,
        scratch_shapes=[pltpu.VMEM((D_CONV // LANES, TAIL_ROWS + tm, LANES), F32)],
        compiler_params=_params(),
        name="in_proj_conv",
    )(x, g, w, w_dw, *to_cast)


_NT = (((1,), (1,)), ((), ()))


def _pair_scores(q2, kw, bias_near, key_lo):
    lane = lax.broadcasted_iota(jnp.int32, q2.shape, 1)
    zero = jnp.zeros_like(q2)
    lhs = jnp.concatenate([jnp.where(lane < ATT_HEAD_DIM, q2, zero), jnp.where(lane < ATT_HEAD_DIM, zero, q2)],
                          axis=0)
    s_far = lax.dot_general(lhs, kw[:FAR_KEYS], _NT, preferred_element_type=F32)
    s_near = lax.dot_general(lhs, kw[FAR_KEYS:], _NT, preferred_element_type=F32) + bias_near
    row = lax.broadcasted_iota(jnp.int32, (2 * PAIR_ROWS, LANES), 0)
    col = lax.broadcasted_iota(jnp.int32, (2 * PAIR_ROWS, LANES), 1)
    s_0 = jnp.where(col >= ((row // CHUNK) % 2) * CHUNK, s_far[:, :LANES], NEG_INF)
    pieces = [s_0, s_far[:, LANES:], s_near]
    if key_lo is not None:
        start = 0
        for n, s in enumerate(pieces):
            c = lax.broadcasted_iota(jnp.int32, s.shape, 1) + start
            pieces[n] = jnp.where(c >= key_lo, s, NEG_INF)
            start += s.shape[1]
    return pieces


def _pair_softmax_pv(pieces, vw):
    blocks = [s[:, c:c + LANES] for s in pieces for c in range(0, s.shape[1], LANES)]
    m = jnp.max(functools.reduce(jnp.maximum, blocks), axis=-1, keepdims=True)
    es = [jnp.exp2(b - m) for b in blocks]
    l = jnp.sum(functools.reduce(jnp.add, es), axis=-1, keepdims=True)
    e = jnp.concatenate([b.astype(BF16) for b in es], axis=-1)
    o = jnp.dot(e, vw, preferred_element_type=F32) / l
    out_lane = lax.broadcasted_iota(jnp.int32, (PAIR_ROWS, LANES), 1)
    return jnp.where(out_lane < ATT_HEAD_DIM, o[:PAIR_ROWS], o[PAIR_ROWS:])


def _band_attn_cached(q, kn, vn, kt_ref, vt_ref, bias_ref):
    t = q.shape[0]
    outs = []
    for h in range(ATT_HEADS):
        sl = slice(h * ATT_HEAD_DIM, (h + 1) * ATT_HEAD_DIM)
        rows = slice((h % 2) * PAIR_ROWS, (h % 2) * PAIR_ROWS + t)
        qh = q[:, sl]
        kt = kt_ref[0, 0, h].astype(BF16)
        vt = vt_ref[0, 0, h].astype(BF16)
        pieces = [
            jnp.dot(qh, kt[:, :FAR_KEYS], preferred_element_type=F32),
            jnp.dot(qh, kt[:, FAR_KEYS:], preferred_element_type=F32)
            + bias_ref[h // 2, rows, :BAND_PAST - FAR_KEYS],
            lax.dot_general(qh, kn[:, sl], _NT, preferred_element_type=F32)
            + bias_ref[h // 2, rows, BAND_PAST - FAR_KEYS:BAND_PAST - FAR_KEYS + t],
        ]
        m = functools.reduce(jnp.maximum, [jnp.max(s, axis=-1, keepdims=True) for s in pieces])
        e_far, e_near, e_new = [jnp.exp2(s - m) for s in pieces]
        l = (jnp.sum(e_far, axis=-1, keepdims=True) + jnp.sum(e_near, axis=-1, keepdims=True)
             + jnp.sum(e_new, axis=-1, keepdims=True))
        o = (lax.dot_general(e_far.astype(BF16), vt[:, :FAR_KEYS], _NT, preferred_element_type=F32)
             + lax.dot_general(e_near.astype(BF16), vt[:, FAR_KEYS:], _NT, preferred_element_type=F32)
             + jnp.dot(e_new.astype(BF16), vn[:, sl], preferred_element_type=F32))
        outs.append(o / l)
    return jnp.concatenate(outs, axis=-1)


def _conv_tail(c, gmat_ref, b_dw_ref, ln_g_ref, ln_b_ref):
    c = c + b_dw_ref[...]
    gm = gmat_ref[...]
    c_hi = c.astype(BF16)
    c_lo = (c - c_hi.astype(F32)).astype(BF16)
    mu = jnp.dot(c_hi, gm, preferred_element_type=F32) + jnp.dot(c_lo, gm, preferred_element_type=F32)
    d = c - mu
    var = jnp.dot((d * d).astype(BF16), gm, preferred_element_type=F32)
    y = d * lax.rsqrt(var + EPS) * ln_g_ref[...] + ln_b_ref[...]
    return y * jax.nn.sigmoid(y)


def _merge(an, cn, x, w_out_ref, g_post_ref):
    y = (jnp.dot(an, w_out_ref[:D_ATTN, :], preferred_element_type=F32)
         + jnp.dot(cn, w_out_ref[D_ATTN:, :], preferred_element_type=F32))
    return x + _rms(y, g_post_ref[...])


def _mix_prompt_kernel(n_cast, q_ref, kp_ref, kc_ref, vp_ref, vc_ref, c_ref, x_ref, bias_ref, gmat_ref,
                       b_dw_ref, ln_g_ref, ln_b_ref, g_a_ref, g_c_ref, w_out_ref, g_post_ref, *refs):
    o_ref = refs[n_cast]
    k_sc, v_sc, a_sc = refs[-3:]
    _cast_slices(refs[:n_cast], refs[n_cast + 1:-3])
    i = pl.program_id(0)

    k_sc[:ROW_TILE, :] = kp_ref[...]
    k_sc[ROW_TILE:, :] = kc_ref[...]
    v_sc[:ROW_TILE, :] = vp_ref[...]
    v_sc[ROW_TILE:, :] = vc_ref[...]

    def scores(r0, p, key_lo):
        ls = slice(p * LANES, (p + 1) * LANES)
        return _pair_scores(q_ref[pl.ds(r0, PAIR_ROWS), ls], k_sc[pl.ds(r0, PAIR_KEYS), ls], bias_ref[p], key_lo)

    def finish(r0, p, pieces):
        ls = slice(p * LANES, (p + 1) * LANES)
        a_sc[pl.ds(r0, PAIR_ROWS), ls] = _pair_softmax_pv(pieces, v_sc[pl.ds(r0, PAIR_KEYS), ls])

    @pl.when(i == 0)
    def _():
        def pair_body(a, carry):
            r0 = pl.multiple_of(a * PAIR_ROWS, PAIR_ROWS)
            for p in range(HEAD_PAIRS):
                finish(r0, p, scores(r0, p, ROW_TILE - r0))
            return carry

        lax.fori_loop(0, ROW_TILE // PAIR_ROWS, pair_body, 0)

    @pl.when(i > 0)
    def _():
        units = [(r0, p) for r0 in range(0, ROW_TILE, PAIR_ROWS) for p in range(HEAD_PAIRS)]
        pieces = scores(*units[0], None)
        for n, (r0, p) in enumerate(units):
            ahead = scores(*units[n + 1], None) if n + 1 < len(units) else None
            finish(r0, p, pieces)
            pieces = ahead

    c = _conv_tail(c_ref[...], gmat_ref, b_dw_ref, ln_g_ref, ln_b_ref)
    an = _rms(a_sc[...], g_a_ref[...]).astype(BF16)
    cn = _rms(c, g_c_ref[...]).astype(BF16)
    o_ref[...] = _merge(an, cn, x_ref[...], w_out_ref, g_post_ref)


def _mix_prompt(q, kb, vb, c, x, bias, gmat, b_dw, ln_g, ln_b, g_a, g_c, w_out, g_post, to_cast=()):
    rows = x.shape[0]
    cur = lambda n: pl.BlockSpec((ROW_TILE, n), lambda i: (i, 0))
    prev = lambda n: pl.BlockSpec((ROW_TILE, n), lambda i: (jnp.maximum(i - 1, 0), 0))
    vec = lambda n: _const_spec((1, n))
    cast_in, cast_out, cast_shape = _cast_specs(to_cast, rows // ROW_TILE)
    return pl.pallas_call(
        functools.partial(_mix_prompt_kernel, len(to_cast)),
        grid=(rows // ROW_TILE,),
        in_specs=[cur(D_ATTN), prev(D_ATTN), cur(D_ATTN), prev(D_ATTN), cur(D_ATTN), cur(D_CONV),
                  cur(D_MODEL), _const_spec(bias.shape), _const_spec(gmat.shape),
                  vec(D_CONV), vec(D_CONV), vec(D_CONV), vec(D_ATTN), vec(D_CONV),
                  _const_spec(w_out.shape), vec(D_MODEL)] + cast_in,
        out_specs=[cur(D_MODEL)] + cast_out,
        out_shape=[jax.ShapeDtypeStruct((rows, D_MODEL), F32)] + cast_shape,
        scratch_shapes=[pltpu.VMEM((2 * ROW_TILE, D_ATTN), BF16), pltpu.VMEM((2 * ROW_TILE, D_ATTN), BF16),
                        pltpu.VMEM((ROW_TILE, D_ATTN), F32)],
        compiler_params=_params(),
        name="mix_prompt",
    )(q, kb, kb, vb, vb, c, x, bias, gmat, b_dw, ln_g, ln_b, g_a, g_c, w_out, g_post, *to_cast)


def _mix_sample_kernel(q_ref, kn_ref, vn_ref, kt_ref, vt_ref, un_ref, cc_ref, x_ref, bias_ref, gmat_ref,
                       w_dw_ref, b_dw_ref, ln_g_ref, ln_b_ref, g_a_ref, g_c_ref, w_out_ref, g_post_ref,
                       o_ref, u_sc, an_sc, cn_sc):
    b = pl.program_id(0)
    t = q_ref.shape[0]
    a = _band_attn_cached(q_ref[...], kn_ref[...], vn_ref[...], kt_ref, vt_ref, bias_ref)

    u_sc[:CONV_WIDTH - 1, :] = cc_ref[0, 0]
    u_sc[CONV_WIDTH - 1:, :] = un_ref[...]
    acc = jnp.zeros((t, D_CONV), F32)
    for j in range(CONV_WIDTH):
        acc = acc + u_sc[j:j + t, :] * w_dw_ref[j:j + 1, :]
    c = _conv_tail(acc, gmat_ref, b_dw_ref, ln_g_ref, ln_b_ref)

    rows = pl.ds(pl.multiple_of(b * t, t), t)
    an_sc[rows, :] = _rms(a, g_a_ref[...]).astype(BF16)
    cn_sc[rows, :] = _rms(c, g_c_ref[...]).astype(BF16)

    @pl.when(b == pl.num_programs(0) - 1)
    def _():
        o_ref[...] = _merge(an_sc[...], cn_sc[...], x_ref[...], w_out_ref, g_post_ref)


def _mix_sample(l, q, kb, vb, u, x, cache_kt, cache_vt, cache_c, bias, gmat, w_dw, b_dw, ln_g, ln_b, g_a, g_c,
                w_out, g_post):
    n_b = cache_kt.shape[1]
    t = x.shape[0] // n_b
    row = lambda n: pl.BlockSpec((t, n), lambda b: (b, 0))
    per_b = lambda a: pl.BlockSpec((1, 1) + a.shape[2:], lambda b: (l, b) + (0,) * (a.ndim - 2))
    vec = lambda n: _const_spec((1, n))
    return pl.pallas_call(
        _mix_sample_kernel,
        grid=(n_b,),
        in_specs=[row(D_ATTN), row(D_ATTN), row(D_ATTN), per_b(cache_kt), per_b(cache_vt), row(D_CONV),
                  per_b(cache_c), _const_spec(x.shape), _const_spec(bias.shape), _const_spec(gmat.shape),
                  _const_spec(w_dw.shape), vec(D_CONV), vec(D_CONV), vec(D_CONV), vec(D_ATTN), vec(D_CONV),
                  _const_spec(w_out.shape), vec(D_MODEL)],
        out_specs=pl.BlockSpec(x.shape, lambda b: (0, 0)),
        out_shape=jax.ShapeDtypeStruct(x.shape, F32),
        scratch_shapes=[pltpu.VMEM((CONV_WIDTH - 1 + t, D_CONV), F32), pltpu.VMEM((x.shape[0], D_ATTN), BF16),
                        pltpu.VMEM((x.shape[0], D_CONV), BF16)],
        compiler_params=_params(),
        name="mix_sample",
    )(q, kb, vb, cache_kt, cache_vt, u, cache_c, x, bias, gmat, w_dw, b_dw, ln_g, ln_b, g_a, g_c, w_out, g_post)


def _mem_kv_kernel(m_ref, g_ref, wk_ref, wv_ref, k_ref, v_ref):
    mn = _rms(m_ref[...], g_ref[...]).astype(BF16)
    k_ref[...] = jnp.dot(mn, wk_ref[...], preferred_element_type=F32)
    v_ref[...] = jnp.dot(mn, wv_ref[...], preferred_element_type=F32)


def _mem_kv(mem, g, wk, wv):
    rows = mem.shape[0]
    full = lambda a: _const_spec(a.shape)
    return pl.pallas_call(
        _mem_kv_kernel,
        grid=(1,),
        in_specs=[full(mem), full(g), full(wk), full(wv)],
        out_specs=[pl.BlockSpec((rows, D_MODEL), lambda i: (0, 0))] * 2,
        out_shape=[jax.ShapeDtypeStruct((rows, D_MODEL), F32)] * 2,
        compiler_params=_params(),
        name="mem_kv",
    )(mem, g, wk, wv)


def _mem_attn_kernel(x_ref, g_pre_ref, wq_ref, mk_ref, mv_ref, wo_ref, g_post_ref, o_ref):
    x = x_ref[...]
    hn = _rms(x, g_pre_ref[...]).astype(BF16)
    q = (jnp.dot(hn, wq_ref[...], preferred_element_type=F32) * (MEM_HEAD_DIM ** -0.5)).astype(BF16)
    mk = mk_ref[0].astype(BF16)
    mv = mv_ref[0].astype(BF16)
    outs = []
    for h in range(MEM_HEADS):
        sl = slice(h * MEM_HEAD_DIM, (h + 1) * MEM_HEAD_DIM)
        s = lax.dot_general(q[:, sl], mk[:, sl], (((1,), (1,)), ((), ())), preferred_element_type=F32)
        outs.append(_softmax_pv(s, mv[:, sl]).astype(BF16))
    o = jnp.concatenate(outs, axis=-1)
    y = jnp.dot(o, wo_ref[...], preferred_element_type=F32)
    o_ref[...] = x + _rms(y, g_post_ref[...])


def _mem_attn(x, g_pre, wq, mk, mv, wo, g_post, tm):
    rows = x.shape[0]
    row = pl.BlockSpec((tm, D_MODEL), lambda i: (i, 0))
    return pl.pallas_call(
        _mem_attn_kernel,
        grid=(rows // tm,),
        in_specs=[row, _const_spec((1, D_MODEL)), _const_spec(wq.shape), _const_spec(mk.shape),
                  _const_spec(mv.shape), _const_spec(wo.shape), _const_spec((1, D_MODEL))],
        out_specs=row,
        out_shape=jax.ShapeDtypeStruct(x.shape, F32),
        compiler_params=_params(),
        name="mem_attn",
    )(x, g_pre, wq, mk, mv, wo, g_post)


def _mem_attn_streams_kernel(t, x_ref, g_pre_ref, wq_ref, mk_ref, mv_ref, wo_ref, g_post_ref, o_ref, q_sc, o_sc):
    b = pl.program_id(0)

    @pl.when(b == 0)
    def _():
        hn = _rms(x_ref[...], g_pre_ref[...]).astype(BF16)
        q = jnp.dot(hn, wq_ref[...], preferred_element_type=F32) * (MEM_HEAD_DIM ** -0.5)
        q_sc[...] = q.astype(BF16)

    mk = mk_ref[0, 0].reshape(MEM_TOKENS * MEM_HEADS, MEM_HEAD_DIM).astype(BF16)
    mv = mv_ref[0, 0].reshape(MEM_TOKENS * MEM_HEADS, MEM_HEAD_DIM).astype(BF16)
    rows = pl.ds(pl.multiple_of(b * t, t), t)
    q = q_sc[rows, :]
    qs = jnp.concatenate([q[:, h * MEM_HEAD_DIM:(h + 1) * MEM_HEAD_DIM] for h in range(MEM_HEADS)], axis=0)
    s = lax.dot_general(qs, mk, _NT, preferred_element_type=F32)
    q_head = lax.broadcasted_iota(jnp.int32, s.shape, 0) // t
    k_head = lax.broadcasted_iota(jnp.int32, s.shape, 1) % MEM_HEADS
    o = _softmax_pv(jnp.where(q_head == k_head, s, NEG_INF), mv)
    o_sc[rows, :] = jnp.concatenate([o[h * t:(h + 1) * t] for h in range(MEM_HEADS)], axis=-1).astype(BF16)

    @pl.when(b == pl.num_programs(0) - 1)
    def _():
        y = jnp.dot(o_sc[...], wo_ref[...], preferred_element_type=F32)
        o_ref[...] = x_ref[...] + _rms(y, g_post_ref[...])


def _mem_attn_streams(l, x, g_pre, wq, cache_mk, cache_mv, wo, g_post):
    n_b = cache_mk.shape[1]
    mem = pl.BlockSpec((1, 1) + cache_mk.shape[2:], lambda b: (l, b, 0, 0, 0))
    return pl.pallas_call(
        functools.partial(_mem_attn_streams_kernel, x.shape[0] // n_b),
        grid=(n_b,),
        in_specs=[_const_spec(x.shape), _const_spec((1, D_MODEL)), _const_spec(wq.shape), mem, mem,
                  _const_spec(wo.shape), _const_spec((1, D_MODEL))],
        out_specs=pl.BlockSpec(x.shape, lambda b: (0, 0)),
        out_shape=jax.ShapeDtypeStruct(x.shape, F32),
        scratch_shapes=[pltpu.VMEM(x.shape, BF16), pltpu.VMEM(x.shape, BF16)],
        compiler_params=_params(),
        name="mem_attn_streams",
    )(x, g_pre, wq, cache_mk, cache_mv, wo, g_post)


def _mlp_kernel(x_ref, g_pre_ref, w_up_ref, w_down_ref, g_post_ref, o_ref):
    x = x_ref[...]
    hn = _rms(x, g_pre_ref[...]).astype(BF16)
    acc = jnp.zeros(x.shape, F32)
    for c in range(D_FF // FF_CHUNK):
        sl = slice(c * FF_CHUNK, (c + 1) * FF_CHUNK)
        h = jnp.maximum(jnp.dot(hn, w_up_ref[:, sl], preferred_element_type=F32), 0.0)
        acc = acc + jnp.dot((h * h).astype(BF16), w_down_ref[sl, :], preferred_element_type=F32)
    o_ref[...] = x + _rms(acc, g_post_ref[...])


def _mlp(x, g_pre, w_up, w_down, g_post, tm):
    rows = x.shape[0]
    row = pl.BlockSpec((tm, D_MODEL), lambda i: (i, 0))
    return pl.pallas_call(
        _mlp_kernel,
        grid=(rows // tm,),
        in_specs=[row, _const_spec((1, D_MODEL)), _const_spec(w_up.shape), _const_spec(w_down.shape),
                  _const_spec((1, D_MODEL))],
        out_specs=row,
        out_shape=jax.ShapeDtypeStruct(x.shape, F32),
        compiler_params=_params(),
        name="mlp",
    )(x, g_pre, w_up, w_down, g_post)


def _group_mean_matrix():
    g = D_CONV // CONV_GROUPS
    return jnp.asarray(np.kron(np.eye(CONV_GROUPS), np.full((g, g), 1.0 / g)), BF16)


def kernel(x_prompt, x_sample, mem_prompt, cache_attn_k, cache_attn_v, cache_conv, cache_mem_k, cache_mem_v, g_mix_pre, w_in, att_rel_bias, w_dw, b_dw, conv_ln_g, conv_ln_b, g_attn_out, g_conv_out, w_out, g_mix_post, g_mem_pre, g_mem_kv, w_mem_q, w_mem_k, w_mem_v, w_mem_o, g_mem_post, g_ffn_pre, w_ffn_up, w_ffn_down, g_ffn_post):
    n_bp, seq, _ = x_prompt.shape
    n_bs, t_new, _ = x_sample.shape
    depth = w_in.shape[0]
    assert n_bp == 1 and seq % ROW_TILE == 0 and ROW_TILE == BAND_PAST
    keep_s = cache_attn_k.shape[2]
    assert keep_s == BAND_PAST and t_new <= CHUNK

    xp = x_prompt.reshape(seq, D_MODEL)
    xs = x_sample.reshape(n_bs * t_new, D_MODEL)
    mem = mem_prompt.reshape(MEM_TOKENS, D_MODEL)
    gmat = _group_mean_matrix()
    cache_kt = jnp.transpose(cache_attn_k, (0, 1, 3, 4, 2))
    cache_vt = jnp.transpose(cache_attn_v, (0, 1, 3, 4, 2))
    vec = lambda a: a.reshape(1, -1)
    p_k, p_v, p_c, p_mk, p_mv, s_k, s_v, s_c = [], [], [], [], [], [], [], []

    for l in range(depth):
        w_in_b = w_in[l].astype(BF16)
        bias = _rel_bias(att_rel_bias[l])

        later = (w_out[l], w_mem_q[l], w_mem_k[l], w_mem_v[l], w_mem_o[l])
        q, kb, vb, c, k_last, v_last, u_last, w_out_b, wq_b, wk_b, wv_b, wo_b = _in_proj(
            xp, vec(g_mix_pre[l]), w_in_b, w_dw[l], ROW_TILE, later)
        conv_w = (vec(b_dw[l]), vec(conv_ln_g[l]), vec(conv_ln_b[l]), vec(g_attn_out[l]), vec(g_conv_out[l]),
                  w_out_b, vec(g_mix_post[l]))
        xp, w_up_b, w_down_b = _mix_prompt(q, kb, vb, c, xp, bias, gmat, *conv_w, (w_ffn_up[l], w_ffn_down[l]))
        p_k.append(k_last.reshape(n_bp, BAND_PAST, ATT_HEADS, ATT_HEAD_DIM))
        p_v.append(v_last.reshape(n_bp, BAND_PAST, ATT_HEADS, ATT_HEAD_DIM))
        p_c.append(u_last[TAIL_ROWS - (CONV_WIDTH - 1):].reshape(n_bp, CONV_WIDTH - 1, D_CONV))

        q, kb, vb, u, k_new, v_new = _in_proj(xs, vec(g_mix_pre[l]), w_in_b, None, n_bs * t_new)
        xs = _mix_sample(l, q, kb, vb, u, xs, cache_kt, cache_vt, cache_conv, bias, gmat, w_dw[l], *conv_w)
        heads = lambda a: a.reshape(n_bs, t_new, ATT_HEADS, ATT_HEAD_DIM)
        s_k.append(jnp.concatenate([cache_attn_k[l], heads(k_new)], axis=1)[:, t_new:])
        s_v.append(jnp.concatenate([cache_attn_v[l], heads(v_new)], axis=1)[:, t_new:])
        s_c.append(jnp.concatenate([cache_conv[l], u.reshape(n_bs, t_new, D_CONV)], axis=1)[:, t_new:])

        mk, mv = _mem_kv(mem, vec(g_mem_kv[l]), wk_b, wv_b)
        p_mk.append(mk.reshape(n_bp, MEM_TOKENS, MEM_HEADS, MEM_HEAD_DIM))
        p_mv.append(mv.reshape(n_bp, MEM_TOKENS, MEM_HEADS, MEM_HEAD_DIM))
        mem_w = (vec(g_mem_pre[l]), wq_b)
        xp = _mem_attn(xp, *mem_w, mk[None], mv[None], wo_b, vec(g_mem_post[l]), ROW_TILE)
        xs = _mem_attn_streams(l, xs, *mem_w, cache_mem_k, cache_mem_v, wo_b, vec(g_mem_post[l]))

        ffn_w = (vec(g_ffn_pre[l]), w_up_b, w_down_b, vec(g_ffn_post[l]))
        xp = _mlp(xp, *ffn_w, ROW_TILE)
        xs = _mlp(xs, *ffn_w, n_bs * t_new)

    return (xp.reshape(x_prompt.shape), xs.reshape(x_sample.shape), jnp.stack(p_k), jnp.stack(p_v),
            jnp.stack(p_c), jnp.stack(p_mk), jnp.stack(p_mv), jnp.stack(s_k), jnp.stack(s_v), jnp.stack(s_c))
```
